```python
import math
import jax, jax.numpy as jnp
from jax import lax
import numpy as np

D_MODEL = 1024
BATCH = 4
SEQ = 4096
DEPTH = 2
DEC_BATCH = 128
DEC_SEQ = 4
PAST_LEN = 2048
PAGE_SIZE = 128

HEAD_DIM = 64
N_HEADS = D_MODEL // (2 * HEAD_DIM)
LAMBDA_INIT = 0.8 - 0.6 * math.exp(-0.3 * 0)
Q_BLOCK = 128
D_CONV = D_MODEL
CONV_WIDTH = 31
D_FF = ((8 * D_MODEL // 3 + 127) // 128) * 128
N_EXPERTS = 8
TOP_K = 2
D_FF_EXPERT = 7 * D_MODEL // 2
EPS = 1e-6

kernel_name = "diffattn_conformer_hybrid_step"


def rms_norm(x, g):
    xf = x.astype(jnp.float32)
    y = xf * lax.rsqrt(jnp.mean(xf * xf, axis=-1, keepdims=True) + EPS)
    return (y * g.astype(jnp.float32)).astype(x.dtype)


def layer_norm(x, g, b):
    xf = x.astype(jnp.float32)
    mu = jnp.mean(xf, axis=-1, keepdims=True)
    xc = xf - mu
    y = xc * lax.rsqrt(jnp.mean(xc * xc, axis=-1, keepdims=True) + EPS)
    return (y * g.astype(jnp.float32) + b.astype(jnp.float32)).astype(x.dtype)


def swiglu(x, w_gate, w_up, w_down):
    return (jax.nn.silu(x @ w_gate) * (x @ w_up)) @ w_down


def moe_swiglu(h, w_router, w_gate_e, w_up_e, w_down_e):
    xt = h.reshape(-1, D_MODEL)
    logits = (xt @ w_router).astype(jnp.float32)
    top_val, top_idx = lax.top_k(logits, TOP_K)
    gates = jax.nn.softmax(top_val, axis=-1)
    combine = jnp.sum(jax.nn.one_hot(top_idx, N_EXPERTS, dtype=jnp.float32) * gates[..., None], axis=1)
    y = jnp.zeros_like(xt)
    for e in range(N_EXPERTS):
        y = y + swiglu(xt, w_gate_e[e], w_up_e[e], w_down_e[e]) * combine[:, e:e + 1].astype(xt.dtype)
    return y.reshape(h.shape)


def diff_lambda(lq1, lk1, lq2, lk2):
    f = lambda a: a.astype(jnp.float32)
    return jnp.exp(jnp.sum(f(lq1) * f(lk1))) - jnp.exp(jnp.sum(f(lq2) * f(lk2))) + LAMBDA_INIT


def qkv_project(h, w_qkv, q_norm, k_norm):
    lead = h.shape[:-1]
    q, k, v = jnp.split(h @ w_qkv, 3, axis=-1)
    q = rms_norm(q.reshape(*lead, N_HEADS, 2, HEAD_DIM), q_norm) * (HEAD_DIM ** -0.5)
    k = rms_norm(k.reshape(*lead, N_HEADS, 2, HEAD_DIM), k_norm)
    v = v.reshape(*lead, N_HEADS, 2 * HEAD_DIM)
    return q, k, v


def diff_weights(s, lam):
    a = jax.nn.softmax(s, axis=-1)
    return a[:, :, 0] - lam * a[:, :, 1]


def attn_output(o, subln_g, w_o):
    o = rms_norm(o, subln_g) * (1.0 - LAMBDA_INIT)
    return o.reshape(*o.shape[:-2], N_HEADS * 2 * HEAD_DIM) @ w_o


def diff_attn_prompt(h, lam, w_qkv, q_norm, k_norm, subln_g, w_o):
    B, T, _ = h.shape
    q, k, v = qkv_project(h, w_qkv, q_norm, k_norm)
    n_blk = T // Q_BLOCK
    qb = q.reshape(B, n_blk, Q_BLOCK, N_HEADS, 2, HEAD_DIM).swapaxes(0, 1)
    key_pos = jnp.arange(T)

    def block(args):
        q_blk, i = args
        s = jnp.einsum('bqhcd,bkhcd->bhcqk', q_blk, k).astype(jnp.float32)
        q_pos = i * Q_BLOCK + jnp.arange(Q_BLOCK)
        mask = key_pos[None, :] <= q_pos[:, None]
        w = diff_weights(jnp.where(mask, s, -jnp.inf), lam)
        return jnp.einsum('bhqk,bkhe->bqhe', w.astype(v.dtype), v)

    o = lax.map(block, (qb, jnp.arange(n_blk)))
    o = o.swapaxes(0, 1).reshape(B, T, N_HEADS, 2 * HEAD_DIM)
    return attn_output(o, subln_g, w_o), k.reshape(B, T, N_HEADS, 2 * HEAD_DIM), v


def diff_attn_sample(h, cache_k, cache_v, page_table, lam, w_qkv, q_norm, k_norm, subln_g, w_o):
    Bd, S, _ = h.shape
    q, k, v = qkv_project(h, w_qkv, q_norm, k_norm)
    past = page_table.shape[1] * PAGE_SIZE
    k_past = cache_k[page_table].reshape(Bd, past, N_HEADS, 2, HEAD_DIM)
    v_past = cache_v[page_table].reshape(Bd, past, N_HEADS, 2 * HEAD_DIM)
    s_past = jnp.einsum('bqhcd,bkhcd->bhcqk', q, k_past.astype(q.dtype)).astype(jnp.float32)
    s_new = jnp.einsum('bqhcd,bkhcd->bhcqk', q, k).astype(jnp.float32)
    causal = jnp.tril(jnp.ones((S, S), dtype=bool))
    s_new = jnp.where(causal, s_new, -jnp.inf)
    w = diff_weights(jnp.concatenate([s_past, s_new], axis=-1), lam).astype(v.dtype)
    o = (jnp.einsum('bhqk,bkhe->bqhe', w[..., :past], v_past.astype(v.dtype))
         + jnp.einsum('bhqk,bkhe->bqhe', w[..., past:], v))
    return attn_output(o, subln_g, w_o), k.reshape(Bd, S, N_HEADS, 2 * HEAD_DIM), v


def conv_module(h, conv_state, w_pw1, b_pw1, w_dw, b_dw, ln_g, ln_b, w_pw2, b_pw2):
    u = h @ w_pw1 + b_pw1
    u = u[..., :D_CONV] * jax.nn.sigmoid(u[..., D_CONV:])
    if conv_state is None:
        u_ext = jnp.pad(u, ((0, 0), (CONV_WIDTH - 1, 0), (0, 0)))
    else:
        u_ext = jnp.concatenate([conv_state.astype(u.dtype), u], axis=1)
    new_state = u_ext[:, -(CONV_WIDTH - 1):]
    c = lax.conv_general_dilated(
        u_ext, w_dw[:, None, :].astype(u_ext.dtype), window_strides=(1,), padding='VALID',
        dimension_numbers=('NWC', 'WIO', 'NWC'), feature_group_count=D_CONV) + b_dw
    c = jax.nn.silu(layer_norm(c, ln_g, ln_b))
    return c @ w_pw2 + b_pw2, new_state


def setup_inputs(seed: int = 0) -> dict:
    key = jax.random.key(seed)
    ks = jax.random.split(key, 40)
    n_pages = PAST_LEN // PAGE_SIZE
    n_phys = (DEC_BATCH * n_pages * 5) // 4
    nrm = lambda k, shape, s: jax.random.normal(k, shape, jnp.float32) * s
    gain = lambda k, n: 1.0 + nrm(k, (n,), 0.01)
    page_table = jax.random.permutation(ks[5], n_phys)[:DEC_BATCH * n_pages].reshape(DEC_BATCH, n_pages).astype(jnp.int32)
    return {
        'x_prompt': nrm(ks[0], (BATCH, SEQ, D_MODEL), 1.0),
        'x_sample': nrm(ks[1], (DEC_BATCH, DEC_SEQ, D_MODEL), 1.0),
        'cache_k': nrm(ks[2], (n_phys, PAGE_SIZE, N_HEADS, 2 * HEAD_DIM), 1.0),
        'cache_v': nrm(ks[3], (n_phys, PAGE_SIZE, N_HEADS, 2 * HEAD_DIM), 1.0),
        'state_conv': nrm(ks[4], (DEC_BATCH, CONV_WIDTH - 1, D_CONV), 1.0),
        'page_table': page_table,
        'norm_attn': gain(ks[6], D_MODEL),
        'w_qkv': nrm(ks[7], (D_MODEL, 3 * N_HEADS * 2 * HEAD_DIM), D_MODEL ** -0.5),
        'q_norm': gain(ks[8], HEAD_DIM),
        'k_norm': gain(ks[9], HEAD_DIM),
        'lambda_q1': nrm(ks[10], (HEAD_DIM,), 0.1),
        'lambda_k1': nrm(ks[11], (HEAD_DIM,), 0.1),
        'lambda_q2': nrm(ks[12], (HEAD_DIM,), 0.1),
        'lambda_k2': nrm(ks[13], (HEAD_DIM,), 0.1),
        'subln_g': gain(ks[14], 2 * HEAD_DIM),
        'w_o': nrm(ks[15], (N_HEADS * 2 * HEAD_DIM, D_MODEL), D_MODEL ** -0.5),
        'norm_ffn0': gain(ks[16], D_MODEL),
        'w_gate0': nrm(ks[17], (D_MODEL, D_FF), D_MODEL ** -0.5),
        'w_up0': nrm(ks[18], (D_MODEL, D_FF), D_MODEL ** -0.5),
        'w_down0': nrm(ks[19], (D_FF, D_MODEL), D_FF ** -0.5),
        'norm_conv': gain(ks[20], D_MODEL),
        'w_pw1': nrm(ks[21], (D_MODEL, 2 * D_CONV), D_MODEL ** -0.5),
        'b_pw1': nrm(ks[22], (2 * D_CONV,), 0.01),
        'w_dw': nrm(ks[23], (CONV_WIDTH, D_CONV), CONV_WIDTH ** -0.5),
        'b_dw': nrm(ks[24], (D_CONV,), 0.01),
        'ln_g': gain(ks[25], D_CONV),
        'ln_b': nrm(ks[26], (D_CONV,), 0.01),
        'w_pw2': nrm(ks[27], (D_CONV, D_MODEL), D_CONV ** -0.5),
        'b_pw2': nrm(ks[28], (D_MODEL,), 0.01),
        'norm_ffn1': gain(ks[29], D_MODEL),
        'w_router': nrm(ks[30], (D_MODEL, N_EXPERTS), D_MODEL ** -0.5),
        'w_gate_e': nrm(ks[31], (N_EXPERTS, D_MODEL, D_FF_EXPERT), D_MODEL ** -0.5),
        'w_up_e': nrm(ks[32], (N_EXPERTS, D_MODEL, D_FF_EXPERT), D_MODEL ** -0.5),
        'w_down_e': nrm(ks[33], (N_EXPERTS, D_FF_EXPERT, D_MODEL), D_FF_EXPERT ** -0.5),
    }


def reference(x_prompt, x_sample, cache_k, cache_v, state_conv, page_table,
              norm_attn, w_qkv, q_norm, k_norm, lambda_q1, lambda_k1, lambda_q2, lambda_k2, subln_g, w_o,
              norm_ffn0, w_gate0, w_up0, w_down0,
              norm_conv, w_pw1, b_pw1, w_dw, b_dw, ln_g, ln_b, w_pw2, b_pw2,
              norm_ffn1, w_router, w_gate_e, w_up_e, w_down_e):
    xp, xs = x_prompt, x_sample
    for layer in range(DEPTH):
        if layer % 2 == 0:
            lam = diff_lambda(lambda_q1, lambda_k1, lambda_q2, lambda_k2)
            yp, k_prompt, v_prompt = diff_attn_prompt(rms_norm(xp, norm_attn), lam, w_qkv, q_norm, k_norm, subln_g, w_o)
            ys, k_sample, v_sample = diff_attn_sample(rms_norm(xs, norm_attn), cache_k, cache_v, page_table, lam,
                                                      w_qkv, q_norm, k_norm, subln_g, w_o)
            xp = xp + yp
            xs = xs + ys
            xp = xp + swiglu(rms_norm(xp, norm_ffn0), w_gate0, w_up0, w_down0)
            xs = xs + swiglu(rms_norm(xs, norm_ffn0), w_gate0, w_up0, w_down0)
        else:
            yp, conv_prompt = conv_module(rms_norm(xp, norm_conv), None, w_pw1, b_pw1, w_dw, b_dw, ln_g, ln_b, w_pw2, b_pw2)
            ys, conv_sample = conv_module(rms_norm(xs, norm_conv), state_conv, w_pw1, b_pw1, w_dw, b_dw, ln_g, ln_b, w_pw2, b_pw2)
            xp = xp + yp
            xs = xs + ys
            xp = xp + moe_swiglu(rms_norm(xp, norm_ffn1), w_router, w_gate_e, w_up_e, w_down_e)
            xs = xs + moe_swiglu(rms_norm(xs, norm_ffn1), w_router, w_gate_e, w_up_e, w_down_e)
    return (xp, xs, k_prompt, v_prompt, k_sample, v_sample, conv_prompt, conv_sample)
```

```python
import functools
import math

import jax
import jax.numpy as jnp
from jax import lax
from jax.experimental import pallas as pl
from jax.experimental.pallas import tpu as pltpu

F32 = jnp.float32
BF16 = jnp.bfloat16

EPS = 1e-6
LAMBDA_INIT = 0.8 - 0.6 * math.exp(-0.3 * 0)
TOP_K = 2
NEG = -1e30
SUBLANES = 8

VMEM_LIMIT_BYTES = 56 * 1024 * 1024

ROW_TILE_QKV = 256
ROW_TILE_FFN = 512
ATTN_TILE = 512
PAGES_PER_STEP = 8
CONV_TILE = 512
CONV_HALO = 32
CONV_ROWS = 16
CONV_SEQ_BLOCK = 32
ROUTER_TILE = 512
MOE_TILE = 1024
MOE_FF_CHUNK = 512


def _cparams(*sem):
    return pltpu.CompilerParams(dimension_semantics=sem, vmem_limit_bytes=VMEM_LIMIT_BYTES)


def _const_spec(shape):
    nd = len(shape)
    return pl.BlockSpec(shape, lambda *_: (0,) * nd, pipeline_mode=pl.Buffered(1))


def _rms(x, g):
    return x * lax.rsqrt(jnp.mean(x * x, axis=-1, keepdims=True) + EPS) * g


def _silu(x):
    return x * jax.nn.sigmoid(x)


def _tile(limit, n):
    t = min(limit, n)
    assert n % t == 0, (n, t)
    return t


def _div_pow2(x, c):
    assert c > 0 and c & (c - 1) == 0, c
    return lax.shift_right_logical(x, c.bit_length() - 1)


def _mod_pow2(x, c):
    assert c > 0 and c & (c - 1) == 0, c
    return lax.bitwise_and(x, c - 1)


def _qkv_body(x_ref, g_ref, w_ref, qn_ref, kn_ref, grp_ref, q_out, k_out, v_out, kb_out, vb_out,
              *, d_model, head_dim):
    x = x_ref[...]
    h = _rms(x, g_ref[...]).astype(BF16)
    qkv = jnp.dot(h, w_ref[...], preferred_element_type=F32)
    q = qkv[:, :d_model]
    k = qkv[:, d_model:2 * d_model]
    v = qkv[:, 2 * d_model:]
    grp = grp_ref[...]

    def head_norm(t, gain):
        ssq = jnp.dot((t * t).astype(BF16), grp, preferred_element_type=F32)
        return t * lax.rsqrt(ssq * (1.0 / head_dim) + EPS) * gain

    qn = head_norm(q, qn_ref[...]) * (head_dim ** -0.5)
    kn = head_norm(k, kn_ref[...])
    q_out[...] = qn.astype(q_out.dtype)
    k_out[...] = kn
    v_out[...] = v
    kb_out[...] = kn.astype(BF16)
    vb_out[...] = v.astype(BF16)


def _qkv(x, g, w_bf, qn_t, kn_t, grp, head_dim, q_dtype):
    n, d = x.shape
    tm = _tile(ROW_TILE_QKV, n)
    row = lambda i: (i, 0)
    out = pl.pallas_call(
        functools.partial(_qkv_body, d_model=d, head_dim=head_dim),
        grid=(n // tm,),
        in_specs=[
            pl.BlockSpec((tm, d), row),
            _const_spec((1, d)),
            _const_spec((d, 3 * d)),
            _const_spec((1, d)),
            _const_spec((1, d)),
            _const_spec((d, d)),
        ],
        out_specs=[pl.BlockSpec((tm, d), row)] * 5,
        out_shape=[
            jax.ShapeDtypeStruct((n, d), q_dtype),
            jax.ShapeDtypeStruct((n, d), F32),
            jax.ShapeDtypeStruct((n, d), F32),
            jax.ShapeDtypeStruct((n, d), BF16),
            jax.ShapeDtypeStruct((n, d), BF16),
        ],
        compiler_params=_cparams("parallel"),
        name="qkv",
    )(x, g, w_bf, qn_t, kn_t, grp)
    return out


def _lambda(lq1, lk1, lq2, lk2):
    a = jnp.sum(lq1[...] * lk1[...], axis=-1, keepdims=True)
    b = jnp.sum(lq2[...] * lk2[...], axis=-1, keepdims=True)
    return jnp.exp(a) - jnp.exp(b) + LAMBDA_INIT


def _attn_p_body(qi_tab, ki_tab, q_ref, k_ref, v_ref, lq1, lk1, lq2, lk2, sg_ref, o_ref,
                 qq_sc, m_sc, l_sc, acc_sc, *, tq, head_dim):
    step = pl.program_id(2)
    qi = qi_tab[step]
    ki = ki_tab[step]

    @pl.when(ki == 0)
    def _():
        q = q_ref[...]
        lane = lax.broadcasted_iota(jnp.int32, q.shape, 1)
        zero = jnp.zeros_like(q)
        qq_sc[0:tq, :] = jnp.where(lane < head_dim, q, zero)
        qq_sc[tq:2 * tq, :] = jnp.where(lane >= head_dim, q, zero)
        m_sc[...] = jnp.full(m_sc.shape, NEG, F32)
        l_sc[...] = jnp.zeros(l_sc.shape, F32)
        acc_sc[...] = jnp.zeros(acc_sc.shape, F32)

    s = lax.dot_general(qq_sc[...], k_ref[...], (((1,), (1,)), ((), ())),
                        preferred_element_type=F32)
    row = lax.broadcasted_iota(jnp.int32, s.shape, 0)
    row = jnp.where(row >= tq, row - tq, row)
    col = lax.broadcasted_iota(jnp.int32, s.shape, 1)
    s = jnp.where(ki * tq + col <= qi * tq + row, s, NEG)
    m_prev = m_sc[...]
    m_new = jnp.maximum(m_prev, jnp.max(s, axis=-1, keepdims=True))
    alpha = jnp.exp(m_prev - m_new)
    p = jnp.exp(s - m_new)
    l_sc[...] = alpha * l_sc[...] + jnp.sum(p, axis=-1, keepdims=True)
    acc_sc[...] = alpha * acc_sc[...] + jnp.dot(p.astype(BF16), v_ref[...],
                                                preferred_element_type=F32)
    m_sc[...] = m_new

    @pl.when(ki == qi)
    def _():
        acc = acc_sc[...]
        l = l_sc[...]
        o1 = acc[0:tq] / l[0:tq]
        o2 = acc[tq:2 * tq] / l[tq:2 * tq]
        o = o1 - _lambda(lq1, lk1, lq2, lk2) * o2
        o_ref[...] = (_rms(o, sg_ref[...]) * (1.0 - LAMBDA_INIT)).astype(o_ref.dtype)


def _attn_prompt(q_bf, k_bf, v_bf, lam_vecs, subln_g, batch, seq, n_heads, head_dim):
    n, d = q_bf.shape
    hw = 2 * head_dim
    tq = _tile(ATTN_TILE, seq)
    nq = seq // tq
    pairs = [(i, j) for i in range(nq) for j in range(i + 1)]
    qi_tab = jnp.asarray([p[0] for p in pairs], jnp.int32)
    ki_tab = jnp.asarray([p[1] for p in pairs], jnp.int32)
    q_map = lambda b, h, s, qt, kt: (b * nq + qt[s], h)
    kv_map = lambda b, h, s, qt, kt: (b * nq + kt[s], h)
    vec = pl.BlockSpec((1, head_dim), lambda b, h, s, qt, kt: (0, 0))
    grid_spec = pltpu.PrefetchScalarGridSpec(
        num_scalar_prefetch=2,
        grid=(batch, n_heads, len(pairs)),
        in_specs=[
            pl.BlockSpec((tq, hw), q_map),
            pl.BlockSpec((tq, hw), kv_map),
            pl.BlockSpec((tq, hw), kv_map),
            vec, vec, vec, vec,
            pl.BlockSpec((1, hw), lambda b, h, s, qt, kt: (0, 0)),
        ],
        out_specs=pl.BlockSpec((tq, hw), q_map),
        scratch_shapes=[
            pltpu.VMEM((2 * tq, hw), BF16),
            pltpu.VMEM((2 * tq, 1), F32),
            pltpu.VMEM((2 * tq, 1), F32),
            pltpu.VMEM((2 * tq, hw), F32),
        ],
    )
    return pl.pallas_call(
        functools.partial(_attn_p_body, tq=tq, head_dim=head_dim),
        grid_spec=grid_spec,
        out_shape=jax.ShapeDtypeStruct((n, d), BF16),
        compiler_params=_cparams("parallel", "parallel", "arbitrary"),
        name="attn_p",
    )(qi_tab, ki_tab, q_bf, k_bf, v_bf, *lam_vecs, subln_g)


def _attn_s_body(pt_ref, q_ref, kn_ref, vn_ref, lq1, lk1, lq2, lk2, sg_ref, *rest,
                 n_pages_step, n_heads, head_dim, s_new):
    k_refs = rest[:n_pages_step]
    v_refs = rest[n_pages_step:2 * n_pages_step]
    o_ref = rest[2 * n_pages_step]
    wq_sc, m_sc, l_sc, acc_sc = rest[2 * n_pages_step + 1:]
    j = pl.program_id(1)
    rows, d = acc_sc.shape
    hw = 2 * head_dim

    @pl.when(j == 0)
    def _():
        ridx = lax.broadcasted_iota(jnp.int32, (rows, d), 0)
        lane = lax.broadcasted_iota(jnp.int32, (rows, d), 1)
        q = q_ref[...]
        own = _div_pow2(lane, head_dim) == _div_pow2(ridx, s_new)
        wq = jnp.zeros((rows, d), F32)
        for t in range(s_new):
            wq = jnp.where(own & (_mod_pow2(ridx, s_new) == t), q[t:t + 1, :], wq)
        wq_sc[...] = wq.astype(BF16)
        qpos = _mod_pow2(lax.broadcasted_iota(jnp.int32, (rows, 1), 0), s_new)
        kn = kn_ref[...]
        vn = vn_ref[...]
        s_cols = []
        for t in range(s_new):
            sc = jnp.sum(wq * kn[t:t + 1, :], axis=-1, keepdims=True)
            s_cols.append(jnp.where(qpos >= t, sc, NEG))
        m = s_cols[0]
        for t in range(1, s_new):
            m = jnp.maximum(m, s_cols[t])
        l = jnp.zeros((rows, 1), F32)
        acc = jnp.zeros((rows, d), F32)
        for t in range(s_new):
            p = jnp.exp(s_cols[t] - m)
            l = l + p
            acc = acc + p * vn[t:t + 1, :]
        m_sc[...] = m
        l_sc[...] = l
        acc_sc[...] = acc

    wq_b = wq_sc[...]
    for i in range(n_pages_step):
        kb = k_refs[i][...].astype(BF16)
        s = lax.dot_general(wq_b, kb, (((1,), (1,)), ((), ())), preferred_element_type=F32)
        m_prev = m_sc[...]
        m_new = jnp.maximum(m_prev, jnp.max(s, axis=-1, keepdims=True))
        alpha = jnp.exp(m_prev - m_new)
        p = jnp.exp(s - m_new)
        l_sc[...] = alpha * l_sc[...] + jnp.sum(p, axis=-1, keepdims=True)
        acc_sc[...] = alpha * acc_sc[...] + jnp.dot(p.astype(BF16), v_refs[i][...].astype(BF16),
                                                    preferred_element_type=F32)
        m_sc[...] = m_new

    @pl.when(j == pl.num_programs(1) - 1)
    def _():
        ridx = lax.broadcasted_iota(jnp.int32, (rows, d), 0)
        lane = lax.broadcasted_iota(jnp.int32, (rows, d), 1)
        accn = acc_sc[...] / l_sc[...]
        per_head = 2 * s_new
        own = jnp.where(_div_pow2(lane, hw) == _div_pow2(ridx, per_head), accn, 0.0)
        o8 = own[0:per_head]
        for h in range(1, n_heads):
            o8 = o8 + own[h * per_head:(h + 1) * per_head]
        o = o8[0:s_new] - _lambda(lq1, lk1, lq2, lk2) * o8[s_new:2 * s_new]
        sg = sg_ref[...]
        for h in range(n_heads):
            oh = o[:, h * hw:(h + 1) * hw]
            o_ref[:, h * hw:(h + 1) * hw] = _rms(oh, sg) * (1.0 - LAMBDA_INIT)


def _attn_sample(q, k_new, v_new, cache_k, cache_v, page_table, lam_vecs, subln_g, n_heads, head_dim):
    bd, s_new, d = q.shape
    n_phys, page, _, hw = cache_k.shape
    ck = cache_k.reshape(n_phys, page, d)
    cv = cache_v.reshape(n_phys, page, d)
    n_pages = page_table.shape[1]
    pps = _tile(PAGES_PER_STEP, n_pages)
    rows = n_heads * 2 * s_new
    seq_spec = pl.BlockSpec((None, s_new, d), lambda b, j, pt: (b, 0, 0))
    vec = pl.BlockSpec((1, head_dim), lambda b, j, pt: (0, 0))

    def page_spec(i):
        return pl.BlockSpec((None, page, d), lambda b, j, pt: (pt[b, j * pps + i], 0, 0))

    grid_spec = pltpu.PrefetchScalarGridSpec(
        num_scalar_prefetch=1,
        grid=(bd, n_pages // pps),
        in_specs=[seq_spec, seq_spec, seq_spec, vec, vec, vec, vec,
                  pl.BlockSpec((1, hw), lambda b, j, pt: (0, 0))]
                 + [page_spec(i) for i in range(pps)] + [page_spec(i) for i in range(pps)],
        out_specs=seq_spec,
        scratch_shapes=[
            pltpu.VMEM((rows, d), BF16),
            pltpu.VMEM((rows, 1), F32),
            pltpu.VMEM((rows, 1), F32),
            pltpu.VMEM((rows, d), F32),
        ],
    )
    return pl.pallas_call(
        functools.partial(_attn_s_body, n_pages_step=pps, n_heads=n_heads, head_dim=head_dim,
                          s_new=s_new),
        grid_spec=grid_spec,
        out_shape=jax.ShapeDtypeStruct((bd, s_new, d), F32),
        compiler_params=_cparams("parallel", "arbitrary"),
        name="attn_s",
    )(page_table, q, k_new, v_new, *lam_vecs, subln_g, *([ck] * pps), *([cv] * pps))


def _ffn0_body(x_ref, o_ref, wo_ref, g_ref, wg_ref, wu_ref, wd_ref, y_ref, *, ff_chunks):
    x1 = x_ref[...] + jnp.dot(o_ref[...].astype(BF16), wo_ref[...], preferred_element_type=F32)
    h = _rms(x1, g_ref[...]).astype(BF16)
    y = x1
    for lo, hi in ff_chunks:
        gate = jnp.dot(h, wg_ref[:, lo:hi], preferred_element_type=F32)
        up = jnp.dot(h, wu_ref[:, lo:hi], preferred_element_type=F32)
        a = (_silu(gate) * up).astype(BF16)
        y = y + jnp.dot(a, wd_ref[lo:hi, :], preferred_element_type=F32)
    y_ref[...] = y


def _ffn0(x, o, wo_bf, g, wg_bf, wu_bf, wd_bf):
    n, d = x.shape
    f = wg_bf.shape[1]
    tm = _tile(ROW_TILE_FFN, n)
    half = (f // 256) * 128
    ff_chunks = ((0, half), (half, f))
    row = lambda i: (i, 0)
    return pl.pallas_call(
        functools.partial(_ffn0_body, ff_chunks=ff_chunks),
        grid=(n // tm,),
        in_specs=[
            pl.BlockSpec((tm, d), row),
            pl.BlockSpec((tm, d), row),
            _const_spec((d, d)),
            _const_spec((1, d)),
            _const_spec((d, f)),
            _const_spec((d, f)),
            _const_spec((f, d)),
        ],
        out_specs=pl.BlockSpec((tm, d), row),
        out_shape=jax.ShapeDtypeStruct((n, d), F32),
        compiler_params=_cparams("parallel"),
        name="ffn0",
    )(x, o, wo_bf, g, wg_bf, wu_bf, wd_bf)


def _glu_pointwise(x, g_ref, w1_ref, b1_ref, d):
    h = _rms(x, g_ref[...]).astype(BF16)
    u = jnp.dot(h, w1_ref[...], preferred_element_type=F32) + b1_ref[...]
    return u[:, :d] * jax.nn.sigmoid(u[:, d:])


def _ln_swish_pointwise(c, lng_ref, lnb_ref, w2_ref, b2_ref):
    mu = jnp.mean(c, axis=-1, keepdims=True)
    xc = c - mu
    yln = xc * lax.rsqrt(jnp.mean(xc * xc, axis=-1, keepdims=True) + EPS) * lng_ref[...] + lnb_ref[...]
    return jnp.dot(_silu(yln).astype(BF16), w2_ref[...], preferred_element_type=F32) + b2_ref[...]


def _conv_p_body(x_ref, g_ref, w1_ref, b1_ref, wdw_ref, bdw_ref, lng_ref, lnb_ref, w2_ref, b2_ref,
                 y_ref, st_ref, u_sc, sh_sc, c_sc, *, tt, width):
    t = pl.program_id(1)
    d = x_ref.shape[-1]
    keep = width - 1

    @pl.when(t == 0)
    def _():
        u_sc[0:CONV_HALO, :] = jnp.zeros((CONV_HALO, d), F32)

    x = x_ref[...]
    u_sc[CONV_HALO:CONV_HALO + tt, :] = _glu_pointwise(x, g_ref, w1_ref, b1_ref, d)

    base = CONV_HALO - keep
    span = sh_sc.shape[1]
    for b in range(1, SUBLANES):
        sh_sc[b - 1, :, :] = u_sc[b:b + span, :]

    def chunk(r, carry):
        r0 = pl.multiple_of(r * CONV_ROWS, CONV_ROWS)
        acc = jnp.zeros((CONV_ROWS, d), F32)
        for j in range(width):
            a, b = divmod(base + j, SUBLANES)
            start = pl.multiple_of(r0 + SUBLANES * a, SUBLANES)
            if b == 0:
                tap = u_sc[pl.ds(start, CONV_ROWS), :]
            else:
                tap = sh_sc[b - 1, pl.ds(start, CONV_ROWS), :]
            acc = acc + wdw_ref[j:j + 1, :] * tap
        c_sc[pl.ds(r0, CONV_ROWS), :] = acc
        return carry

    lax.fori_loop(0, tt // CONV_ROWS, chunk, 0)

    c = c_sc[...] + bdw_ref[...]
    y_ref[...] = x + _ln_swish_pointwise(c, lng_ref, lnb_ref, w2_ref, b2_ref)
    st_ref[...] = u_sc[CONV_HALO + tt - keep:CONV_HALO + tt, :]
    u_sc[0:CONV_HALO, :] = u_sc[tt:tt + CONV_HALO, :]


def _conv_prompt(x, batch, seq, g, w1_bf, b1, wdw, bdw, lng, lnb, w2_bf, b2):
    n, d = x.shape
    width = wdw.shape[0]
    tt = _tile(CONV_TILE, seq)
    nt = seq // tt
    row = lambda b, t: (b * nt + t, 0)
    y, st = pl.pallas_call(
        functools.partial(_conv_p_body, tt=tt, width=width),
        grid=(batch, nt),
        in_specs=[
            pl.BlockSpec((tt, d), row),
            _const_spec((1, d)),
            _const_spec((d, 2 * d)),
            _const_spec((1, 2 * d)),
            _const_spec((width, d)),
            _const_spec((1, d)),
            _const_spec((1, d)),
            _const_spec((1, d)),
            _const_spec((d, d)),
            _const_spec((1, d)),
        ],
        out_specs=[
            pl.BlockSpec((tt, d), row),
            pl.BlockSpec((None, width - 1, d), lambda b, t: (b, 0, 0)),
        ],
        out_shape=[
            jax.ShapeDtypeStruct((n, d), F32),
            jax.ShapeDtypeStruct((batch, width - 1, d), F32),
        ],
        scratch_shapes=[
            pltpu.VMEM((CONV_HALO + tt, d), F32),
            pltpu.VMEM((SUBLANES - 1, CONV_HALO + tt - SUBLANES, d), F32),
            pltpu.VMEM((tt, d), F32),
        ],
        compiler_params=_cparams("parallel", "arbitrary"),
        name="conv_p",
    )(x, g, w1_bf, b1, wdw, bdw, lng, lnb, w2_bf, b2)
    return y, st


def _conv_s_body(x_ref, st_in_ref, g_ref, w1_ref, b1_ref, wdw_ref, bdw_ref, lng_ref, lnb_ref,
                 w2_ref, b2_ref, y_ref, st_out_ref, ext_sc, c_sc, *, s_new, width, n_seq):
    d = x_ref.shape[-1]
    keep = width - 1
    x = x_ref[...]
    ug = _glu_pointwise(x, g_ref, w1_ref, b1_ref, d)
    ext_sc[:, 0:keep, :] = st_in_ref[...]
    for s in range(n_seq):
        ext_sc[s, keep:keep + s_new, :] = ug[s * s_new:(s + 1) * s_new, :]
    acc = jnp.zeros((n_seq, s_new, d), F32)
    for j in range(width):
        acc = acc + wdw_ref[j:j + 1, :][None] * ext_sc[:, j:j + s_new, :]
    for s in range(n_seq):
        c_sc[s * s_new:(s + 1) * s_new, :] = acc[s]
    st_out_ref[...] = ext_sc[:, s_new:s_new + keep, :]
    c = c_sc[...] + bdw_ref[...]
    y_ref[...] = x + _ln_swish_pointwise(c, lng_ref, lnb_ref, w2_ref, b2_ref)


def _conv_sample(x, state, s_new, g, w1_bf, b1, wdw, bdw, lng, lnb, w2_bf, b2):
    n, d = x.shape
    bd = state.shape[0]
    width = wdw.shape[0]
    keep = width - 1
    ns = _tile(CONV_SEQ_BLOCK, bd)
    y, st = pl.pallas_call(
        functools.partial(_conv_s_body, s_new=s_new, width=width, n_seq=ns),
        grid=(bd // ns,),
        in_specs=[
            pl.BlockSpec((ns * s_new, d), lambda i: (i, 0)),
            pl.BlockSpec((ns, keep, d), lambda i: (i, 0, 0)),
            _const_spec((1, d)),
            _const_spec((d, 2 * d)),
            _const_spec((1, 2 * d)),
            _const_spec((width, d)),
            _const_spec((1, d)),
            _const_spec((1, d)),
            _const_spec((1, d)),
            _const_spec((d, d)),
            _const_spec((1, d)),
        ],
        out_specs=[
            pl.BlockSpec((ns * s_new, d), lambda i: (i, 0)),
            pl.BlockSpec((ns, keep, d), lambda i: (i, 0, 0)),
        ],
        out_shape=[
            jax.ShapeDtypeStruct((n, d), F32),
            jax.ShapeDtypeStruct((bd, keep, d), F32),
        ],
        scratch_shapes=[
            pltpu.VMEM((ns, keep + s_new, d), F32),
            pltpu.VMEM((ns * s_new, d), F32),
        ],
        compiler_params=_cparams("parallel"),
        name="conv_s",
    )(x, state, g, w1_bf, b1, wdw, bdw, lng, lnb, w2_bf, b2)
    return y, st


def _router_body(x_ref, g_ref, wrt_ref, h_ref, idx_ref, gate_ref):
    h = _rms(x_ref[...], g_ref[...])
    h_ref[...] = h.astype(BF16)
    logits = lax.dot_general(wrt_ref[...], h, (((1,), (1,)), ((), ())),
                             precision=lax.Precision.HIGHEST,
                             preferred_element_type=F32)
    n_e = float(logits.shape[0])
    eidx = lax.broadcasted_iota(jnp.int32, logits.shape, 0).astype(F32)
    v1 = jnp.max(logits, axis=0, keepdims=True)
    i1 = jnp.min(jnp.where(logits == v1, eidx, n_e), axis=0, keepdims=True)
    rest = jnp.where(eidx == i1, -jnp.inf, logits)
    v2 = jnp.max(rest, axis=0, keepdims=True)
    i2 = jnp.min(jnp.where(rest == v2, eidx, n_e), axis=0, keepdims=True)
    e = jnp.exp(v2 - v1)
    g1 = 1.0 / (1.0 + e)
    idx_ref[0:1, :] = i1.astype(jnp.int32)
    idx_ref[1:2, :] = i2.astype(jnp.int32)
    gate_ref[0:1, :] = g1
    gate_ref[1:2, :] = e * g1


def _router(x, g, w_router_t):
    n, d = x.shape
    n_e = w_router_t.shape[0]
    tm = _tile(ROUTER_TILE, n)
    return pl.pallas_call(
        _router_body,
        grid=(n // tm,),
        in_specs=[
            pl.BlockSpec((tm, d), lambda i: (i, 0)),
            _const_spec((1, d)),
            _const_spec((n_e, d)),
        ],
        out_specs=[
            pl.BlockSpec((tm, d), lambda i: (i, 0)),
            pl.BlockSpec((TOP_K, tm), lambda i: (0, i)),
            pl.BlockSpec((TOP_K, tm), lambda i: (0, i)),
        ],
        out_shape=[
            jax.ShapeDtypeStruct((n, d), BF16),
            jax.ShapeDtypeStruct((TOP_K, n), jnp.int32),
            jax.ShapeDtypeStruct((TOP_K, n), F32),
        ],
        compiler_params=_cparams("parallel"),
        name="router",
    )(x, g, w_router_t)


def _moe_body(te_ref, tv_ref, ts_ref, x_ref, gt_ref, wg_ref, wu_ref, wd_ref, y_ref, acc_sc):
    t = pl.program_id(0)
    c = pl.program_id(1)

    @pl.when(tv_ref[t] == 1)
    def _():
        @pl.when(c == 0)
        def _():
            acc_sc[...] = jnp.zeros(acc_sc.shape, F32)

        x = x_ref[...]
        gate = jnp.dot(x, wg_ref[...].astype(BF16), preferred_element_type=F32)
        up = jnp.dot(x, wu_ref[...].astype(BF16), preferred_element_type=F32)
        a = (_silu(gate) * up).astype(BF16)
        acc_sc[...] += jnp.dot(a, wd_ref[...].astype(BF16), preferred_element_type=F32)

        @pl.when(c == pl.num_programs(1) - 1)
        def _():
            y_ref[...] = acc_sc[...] * gt_ref[...]


def _moe(xs, gate_slot, tile_expert, tile_valid, tile_src, w_gate_e, w_up_e, w_down_e, tm):
    p_rows, d = xs.shape
    n_e, _, f = w_gate_e.shape
    fc = _tile(MOE_FF_CHUNK, f)
    nc = f // fc
    n_tiles = p_rows // tm

    def chunk_of(t, c, tv):
        return jnp.where(tv[t] == 1, c, nc - 1)

    grid_spec = pltpu.PrefetchScalarGridSpec(
        num_scalar_prefetch=3,
        grid=(n_tiles, nc),
        in_specs=[
            pl.BlockSpec((tm, d), lambda t, c, te, tv, ts: (ts[t], 0)),
            pl.BlockSpec((tm, 1), lambda t, c, te, tv, ts: (ts[t], 0)),
            pl.BlockSpec((None, d, fc), lambda t, c, te, tv, ts: (te[t], 0, chunk_of(t, c, tv))),
            pl.BlockSpec((None, d, fc), lambda t, c, te, tv, ts: (te[t], 0, chunk_of(t, c, tv))),
            pl.BlockSpec((None, fc, d), lambda t, c, te, tv, ts: (te[t], chunk_of(t, c, tv), 0)),
        ],
        out_specs=pl.BlockSpec((tm, d), lambda t, c, te, tv, ts: (ts[t], 0)),
        scratch_shapes=[pltpu.VMEM((tm, d), F32)],
    )
    return pl.pallas_call(
        _moe_body,
        grid_spec=grid_spec,
        out_shape=jax.ShapeDtypeStruct((p_rows, d), F32),
        compiler_params=_cparams("arbitrary", "arbitrary"),
        name="moe",
    )(tile_expert, tile_valid, tile_src, xs, gate_slot, w_gate_e, w_up_e, w_down_e)


def _moe_layer(x, g, w_router, w_gate_e, w_up_e, w_down_e):
    n, d = x.shape
    n_e = w_router.shape[1]
    tm = min(MOE_TILE, n)
    h_bf, idx, gates = _router(x, g, w_router.T)

    e_flat = idx.reshape(-1)
    onehot = (e_flat[:, None] == jnp.arange(n_e, dtype=jnp.int32)[None, :]).astype(jnp.int32)
    csum = jnp.cumsum(onehot, axis=0)
    rank = jnp.sum(onehot * csum, axis=1) - 1
    counts = csum[-1]
    padded = ((counts + tm - 1) // tm) * tm
    ends = jnp.cumsum(padded)
    dest = (ends - padded)[e_flat] + rank
    n_tiles = (TOP_K * n) // tm + n_e
    p_rows = n_tiles * tm
    tile_start = jnp.arange(n_tiles, dtype=jnp.int32) * tm
    tile_valid = (tile_start < ends[-1]).astype(jnp.int32)
    last_valid = jnp.maximum(jnp.sum(tile_valid) - 1, 0)
    tile_src = jnp.minimum(jnp.arange(n_tiles, dtype=jnp.int32), last_valid)
    tile_expert = jnp.minimum(
        jnp.searchsorted(ends, tile_src * tm, side="right"), n_e - 1).astype(jnp.int32)
    token = jnp.tile(jnp.arange(n, dtype=jnp.int32), TOP_K)
    src_token = jnp.zeros((p_rows,), jnp.int32).at[dest].set(token)
    gate_slot = jnp.zeros((p_rows,), F32).at[dest].set(gates.reshape(-1))

    xs = jnp.take(h_bf, src_token, axis=0)
    ys = _moe(xs, gate_slot[:, None], tile_expert, tile_valid, tile_src,
              w_gate_e, w_up_e, w_down_e, tm)
    y0 = jnp.take(ys, dest[:n], axis=0)
    y1 = jnp.take(ys, dest[n:], axis=0)
    return x + (y0 + y1)


def kernel(x_prompt, x_sample, cache_k, cache_v, state_conv, page_table, norm_attn, w_qkv, q_norm, k_norm, lambda_q1, lambda_k1, lambda_q2, lambda_k2, subln_g, w_o, norm_ffn0, w_gate0, w_up0, w_down0, norm_conv, w_pw1, b_pw1, w_dw, b_dw, ln_g, ln_b, w_pw2, b_pw2, norm_ffn1, w_router, w_gate_e, w_up_e, w_down_e):
    batch, seq, d = x_prompt.shape
    bd, s_new, _ = x_sample.shape
    n_heads = cache_k.shape[2]
    head_dim = q_norm.shape[0]
    hw = 2 * head_dim
    n_p = batch * seq
    n_s = bd * s_new

    row = lambda v: v.reshape(1, -1)
    bf = lambda w: w.astype(BF16)
    xp = x_prompt.reshape(n_p, d)
    xs = x_sample.reshape(n_s, d)
    grp = jnp.kron(jnp.eye(d // head_dim, dtype=F32), jnp.ones((head_dim, head_dim), F32)).astype(BF16)
    qn_t = row(jnp.tile(q_norm, d // head_dim))
    kn_t = row(jnp.tile(k_norm, d // head_dim))
    lam_vecs = (row(lambda_q1), row(lambda_k1), row(lambda_q2), row(lambda_k2))
    sg = row(subln_g)
    w_qkv_bf = bf(w_qkv)

    qp, kp, vp, kpb, vpb = _qkv(xp, row(norm_attn), w_qkv_bf, qn_t, kn_t, grp, head_dim, BF16)
    qs, ks, vs, _, _ = _qkv(xs, row(norm_attn), w_qkv_bf, qn_t, kn_t, grp, head_dim, F32)
    op = _attn_prompt(qp, kpb, vpb, lam_vecs, sg, batch, seq, n_heads, head_dim)
    osm = _attn_sample(qs.reshape(bd, s_new, d), ks.reshape(bd, s_new, d), vs.reshape(bd, s_new, d),
                       cache_k, cache_v, page_table, lam_vecs, sg, n_heads, head_dim)

    ffn0_w = (bf(w_o), row(norm_ffn0), bf(w_gate0), bf(w_up0), bf(w_down0))
    xp = _ffn0(xp, op, *ffn0_w)
    xs = _ffn0(xs, osm.reshape(n_s, d), *ffn0_w)

    conv_w = (row(norm_conv), bf(w_pw1), row(b_pw1), w_dw, row(b_dw), row(ln_g), row(ln_b),
              bf(w_pw2), row(b_pw2))
    xp, conv_prompt = _conv_prompt(xp, batch, seq, *conv_w)
    xs, conv_sample = _conv_sample(xs, state_conv, s_new, *conv_w)

    x_all = _moe_layer(jnp.concatenate([xp, xs], axis=0), row(norm_ffn1), w_router,
                       w_gate_e, w_up_e, w_down_e)

    return (x_all[:n_p].reshape(batch, seq, d), x_all[n_p:].reshape(bd, s_new, d),
            kp.reshape(batch, seq, n_heads, hw), vp.reshape(batch, seq, n_heads, hw),
            ks.reshape(bd, s_new, n_heads, hw), vs.reshape(bd, s_new, n_heads, hw),
            conv_prompt, conv_sample)
```

```python
import functools
import math

import jax
import jax.numpy as jnp
from jax import lax
from jax.experimental import pallas as pl
from jax.experimental.pallas import tpu as pltpu

F32 = jnp.float32
BF16 = jnp.bfloat16

EPS = 1e-6
LAMBDA_INIT = 0.8 - 0.6 * math.exp(-0.3 * 0)
TOP_K = 2
NEG = -1e30
SUBLANES = 8

VMEM_LIMIT_BYTES = 56 * 1024 * 1024

ROW_TILE_QKV = 256
ROW_TILE_FFN = 512
ATTN_TILE = 512
ATTN_HEADS_PER_STEP = 2
ATTN_ROW_CHUNK = 256
PAGES_PER_STEP = 8
CONV_TILE = 512
CONV_HALO = 32
CONV_ROWS = 16
CONV_SEQ_BLOCK = 32
ROUTER_TILE = 512
MOE_TILE = 1024
MOE_FF_CHUNK = 512


def _cparams(*sem):
    return pltpu.CompilerParams(dimension_semantics=sem, vmem_limit_bytes=VMEM_LIMIT_BYTES)


def _const_spec(shape):
    nd = len(shape)
    return pl.BlockSpec(shape, lambda *_: (0,) * nd, pipeline_mode=pl.Buffered(1))


def _rms(x, g):
    return x * lax.rsqrt(jnp.mean(x * x, axis=-1, keepdims=True) + EPS) * g


def _silu(x):
    return x * jax.nn.sigmoid(x)


def _tile(limit, n):
    t = min(limit, n)
    assert n % t == 0, (n, t)
    return t


def _div_pow2(x, c):
    assert c > 0 and c & (c - 1) == 0, c
    return lax.shift_right_logical(x, c.bit_length() - 1)


def _mod_pow2(x, c):
    assert c > 0 and c & (c - 1) == 0, c
    return lax.bitwise_and(x, c - 1)


def _qkv_body(x_ref, g_ref, w_ref, qn_ref, kn_ref, grp_ref, q_out, k_out, v_out, kb_out, vb_out,
              *, d_model, head_dim, q_scale):
    x = x_ref[...]
    h = _rms(x, g_ref[...]).astype(BF16)
    qkv = jnp.dot(h, w_ref[...], preferred_element_type=F32)
    q = qkv[:, :d_model]
    k = qkv[:, d_model:2 * d_model]
    v = qkv[:, 2 * d_model:]
    grp = grp_ref[...]

    def head_norm(t, gain):
        ssq = jnp.dot((t * t).astype(BF16), grp, preferred_element_type=F32)
        return t * lax.rsqrt(ssq * (1.0 / head_dim) + EPS) * gain

    qn = head_norm(q, qn_ref[...]) * q_scale
    kn = head_norm(k, kn_ref[...])
    q_out[...] = qn.astype(q_out.dtype)
    k_out[...] = kn
    v_out[...] = v
    kb_out[...] = kn.astype(BF16)
    vb_out[...] = v.astype(BF16)


def _qkv(x, g, w_bf, qn_t, kn_t, grp, head_dim, q_dtype, q_scale):
    n, d = x.shape
    tm = _tile(ROW_TILE_QKV, n)
    row = lambda i: (i, 0)
    out = pl.pallas_call(
        functools.partial(_qkv_body, d_model=d, head_dim=head_dim, q_scale=q_scale),
        grid=(n // tm,),
        in_specs=[
            pl.BlockSpec((tm, d), row),
            _const_spec((1, d)),
            _const_spec((d, 3 * d)),
            _const_spec((1, d)),
            _const_spec((1, d)),
            _const_spec((d, d)),
        ],
        out_specs=[pl.BlockSpec((tm, d), row)] * 5,
        out_shape=[
            jax.ShapeDtypeStruct((n, d), q_dtype),
            jax.ShapeDtypeStruct((n, d), F32),
            jax.ShapeDtypeStruct((n, d), F32),
            jax.ShapeDtypeStruct((n, d), BF16),
            jax.ShapeDtypeStruct((n, d), BF16),
        ],
        compiler_params=_cparams("parallel"),
        name="qkv",
    )(x, g, w_bf, qn_t, kn_t, grp)
    return out


def _lambda(lq1, lk1, lq2, lk2):
    a = jnp.sum(lq1[...] * lk1[...], axis=-1, keepdims=True)
    b = jnp.sum(lq2[...] * lk2[...], axis=-1, keepdims=True)
    return jnp.exp(a) - jnp.exp(b) + LAMBDA_INIT


def _attn_p_body(qi_tab, ki_tab, q_ref, k_ref, v_ref, lq1, lk1, lq2, lk2, sg_ref, o_ref,
                 qq_sc, m_sc, l_sc, acc_sc, *, tq, head_dim, heads):
    step = pl.program_id(2)
    qi = qi_tab[step]
    ki = ki_tab[step]
    hw = 2 * head_dim
    rc = _tile(ATTN_ROW_CHUNK, tq)

    @pl.when(ki == 0)
    def _():
        for h in range(heads):
            q = q_ref[:, h * hw:(h + 1) * hw]
            lane = lax.broadcasted_iota(jnp.int32, q.shape, 1)
            zero = jnp.zeros_like(q)
            qq_sc[h, 0:tq, :] = jnp.where(lane < head_dim, q, zero)
            qq_sc[h, tq:2 * tq, :] = jnp.where(lane >= head_dim, q, zero)
        m_sc[...] = jnp.full(m_sc.shape, NEG, F32)
        l_sc[...] = jnp.zeros(l_sc.shape, F32)
        acc_sc[...] = jnp.zeros(acc_sc.shape, F32)

    def update(diagonal):
        for h in range(heads):
            lanes = slice(h * hw, (h + 1) * hw)
            for r0 in range(0, 2 * tq, rc):
                q0 = r0 % tq
                nk = q0 + rc if diagonal else tq
                rows = slice(r0, r0 + rc)
                s = lax.dot_general(qq_sc[h, rows, :], k_ref[0:nk, lanes], (((1,), (1,)), ((), ())),
                                    preferred_element_type=F32)
                if diagonal:
                    row = lax.broadcasted_iota(jnp.int32, s.shape, 0) + q0
                    col = lax.broadcasted_iota(jnp.int32, s.shape, 1)
                    s = jnp.where(col <= row, s, NEG)
                m_prev = m_sc[h, rows, :]
                m_new = jnp.maximum(m_prev, jnp.max(s, axis=-1, keepdims=True))
                alpha = jnp.exp2(m_prev - m_new)
                p = jnp.exp2(s - jnp.concatenate([m_new] * (nk // hw), axis=1)).astype(BF16)
                v1 = jnp.concatenate([v_ref[0:nk, lanes], jnp.ones((nk, hw), BF16)], axis=1)
                pv = jnp.dot(p, v1, preferred_element_type=F32)
                l_sc[h, rows, :] = alpha * l_sc[h, rows, :] + pv[:, hw:]
                acc_sc[h, rows, :] = alpha * acc_sc[h, rows, :] + pv[:, :hw]
                m_sc[h, rows, :] = m_new

    @pl.when(ki < qi)
    def _():
        update(False)

    @pl.when(ki == qi)
    def _():
        update(True)
        lam = _lambda(lq1, lk1, lq2, lk2)
        for h in range(heads):
            o1 = acc_sc[h, 0:tq, :] / l_sc[h, 0:tq, :]
            o2 = acc_sc[h, tq:2 * tq, :] / l_sc[h, tq:2 * tq, :]
            o = o1 - lam * o2
            o_ref[:, h * hw:(h + 1) * hw] = (_rms(o, sg_ref[...]) * (1.0 - LAMBDA_INIT)).astype(o_ref.dtype)


def _attn_prompt(q_bf, k_bf, v_bf, lam_vecs, subln_g, batch, seq, n_heads, head_dim):
    n, d = q_bf.shape
    hw = 2 * head_dim
    tq = _tile(ATTN_TILE, seq)
    nq = seq // tq
    heads = _tile(ATTN_HEADS_PER_STEP, n_heads)
    pairs = [(i, j) for i in range(nq) for j in range(i + 1)]
    qi_tab = jnp.asarray([p[0] for p in pairs], jnp.int32)
    ki_tab = jnp.asarray([p[1] for p in pairs], jnp.int32)
    q_map = lambda b, h, s, qt, kt: (b * nq + qt[s], h)
    kv_map = lambda b, h, s, qt, kt: (b * nq + kt[s], h)
    vec = pl.BlockSpec((1, head_dim), lambda b, h, s, qt, kt: (0, 0))
    grid_spec = pltpu.PrefetchScalarGridSpec(
        num_scalar_prefetch=2,
        grid=(batch, n_heads // heads, len(pairs)),
        in_specs=[
            pl.BlockSpec((tq, heads * hw), q_map),
            pl.BlockSpec((tq, heads * hw), kv_map),
            pl.BlockSpec((tq, heads * hw), kv_map),
            vec, vec, vec, vec,
            pl.BlockSpec((1, hw), lambda b, h, s, qt, kt: (0, 0)),
        ],
        out_specs=pl.BlockSpec((tq, heads * hw), q_map),
        scratch_shapes=[
            pltpu.VMEM((heads, 2 * tq, hw), BF16),
            pltpu.VMEM((heads, 2 * tq, hw), F32),
            pltpu.VMEM((heads, 2 * tq, hw), F32),
            pltpu.VMEM((heads, 2 * tq, hw), F32),
        ],
    )
    return pl.pallas_call(
        functools.partial(_attn_p_body, tq=tq, head_dim=head_dim, heads=heads),
        grid_spec=grid_spec,
        out_shape=jax.ShapeDtypeStruct((n, d), BF16),
        compiler_params=_cparams("parallel", "parallel", "arbitrary"),
        name="attn_p",
    )(qi_tab, ki_tab, q_bf, k_bf, v_bf, *lam_vecs, subln_g)


def _attn_s_body(pt_ref, q_ref, kn_ref, vn_ref, lq1, lk1, lq2, lk2, sg_ref, *rest,
                 n_pages_step, n_heads, head_dim, s_new):
    k_refs = rest[:n_pages_step]
    v_refs = rest[n_pages_step:2 * n_pages_step]
    o_ref = rest[2 * n_pages_step]
    qm_sc, bias_sc, m_sc, l_sc, acc_sc = rest[2 * n_pages_step + 1:]
    j = pl.program_id(1)
    rows, hw = acc_sc.shape
    per_head = 2 * s_new
    nt = (((1,), (1,)), ((), ()))

    @pl.when(j == 0)
    def _():
        q = q_ref[...]
        r8 = lax.broadcasted_iota(jnp.int32, (per_head, hw), 0)
        l8 = lax.broadcasted_iota(jnp.int32, (per_head, hw), 1)
        own_map = _div_pow2(l8, head_dim) == _div_pow2(r8, s_new)
        blocks = []
        for h in range(n_heads):
            qh = q[:, h * hw:(h + 1) * hw]
            blk = jnp.zeros((per_head, hw), F32)
            for t in range(s_new):
                blk = jnp.where(own_map & (_mod_pow2(r8, s_new) == t), qh[t:t + 1, :], blk)
            blocks.append(blk)
        qm = jnp.concatenate(blocks, axis=0).astype(BF16)
        qm_sc[...] = qm
        ridx = lax.broadcasted_iota(jnp.int32, bias_sc.shape, 0)
        col = lax.broadcasted_iota(jnp.int32, bias_sc.shape, 1)
        bias_sc[...] = jnp.where(_mod_pow2(col, n_heads) == _div_pow2(ridx, per_head), 0.0, NEG)
        n_new = kn_ref.shape[0]
        s = lax.dot_general(qm, kn_ref[...].astype(BF16), nt, preferred_element_type=F32)
        ridx = lax.broadcasted_iota(jnp.int32, (rows, n_new), 0)
        col = lax.broadcasted_iota(jnp.int32, (rows, n_new), 1)
        ok = (_mod_pow2(col, n_heads) == _div_pow2(ridx, per_head)) & (
            _div_pow2(col, n_heads) <= _mod_pow2(ridx, s_new))
        s = jnp.where(ok, s, NEG)
        m = jnp.max(s, axis=-1, keepdims=True)
        p = jnp.exp(s - m)
        m_sc[...] = m
        l_sc[...] = jnp.sum(p, axis=-1, keepdims=True)
        acc_sc[...] = jnp.dot(p.astype(BF16), vn_ref[...].astype(BF16), preferred_element_type=F32)

    qm = qm_sc[...]
    bias = bias_sc[...]
    scores = [lax.dot_general(qm, k_refs[i][...].astype(BF16), nt, preferred_element_type=F32) + bias
              for i in range(n_pages_step)]
    m_prev = m_sc[...]
    m_new = m_prev
    for s in scores:
        m_new = jnp.maximum(m_new, jnp.max(s, axis=-1, keepdims=True))
    alpha = jnp.exp(m_prev - m_new)
    l = alpha * l_sc[...]
    acc = alpha * acc_sc[...]
    for i, s in enumerate(scores):
        p = jnp.exp(s - m_new)
        l = l + jnp.sum(p, axis=-1, keepdims=True)
        acc = acc + jnp.dot(p.astype(BF16), v_refs[i][...].astype(BF16), preferred_element_type=F32)
    m_sc[...] = m_new
    l_sc[...] = l
    acc_sc[...] = acc

    @pl.when(j == pl.num_programs(1) - 1)
    def _():
        accn = acc / l
        lam = _lambda(lq1, lk1, lq2, lk2)
        sg = sg_ref[...]
        for h in range(n_heads):
            blk = accn[h * per_head:(h + 1) * per_head]
            oh = blk[0:s_new] - lam * blk[s_new:per_head]
            o_ref[:, h * hw:(h + 1) * hw] = _rms(oh, sg) * (1.0 - LAMBDA_INIT)


def _attn_sample(q, k_new, v_new, cache_k, cache_v, page_table, lam_vecs, subln_g, n_heads, head_dim):
    bd, s_new, d = q.shape
    n_phys, page, _, hw = cache_k.shape
    prow = page * n_heads
    ck = cache_k.reshape(n_phys, prow, hw)
    cv = cache_v.reshape(n_phys, prow, hw)
    n_pages = page_table.shape[1]
    pps = _tile(PAGES_PER_STEP, n_pages)
    rows = n_heads * 2 * s_new
    seq_spec = pl.BlockSpec((None, s_new, d), lambda b, j, pt: (b, 0, 0))
    new_spec = pl.BlockSpec((None, s_new * n_heads, hw), lambda b, j, pt: (b, 0, 0))
    vec = pl.BlockSpec((1, head_dim), lambda b, j, pt: (0, 0))

    def page_spec(i):
        return pl.BlockSpec((None, prow, hw), lambda b, j, pt: (pt[b, j * pps + i], 0, 0))

    grid_spec = pltpu.PrefetchScalarGridSpec(
        num_scalar_prefetch=1,
        grid=(bd, n_pages // pps),
        in_specs=[seq_spec, new_spec, new_spec, vec, vec, vec, vec,
                  pl.BlockSpec((1, hw), lambda b, j, pt: (0, 0))]
                 + [page_spec(i) for i in range(pps)] + [page_spec(i) for i in range(pps)],
        out_specs=seq_spec,
        scratch_shapes=[
            pltpu.VMEM((rows, hw), BF16),
            pltpu.VMEM((rows, prow), F32),
            pltpu.VMEM((rows, 1), F32),
            pltpu.VMEM((rows, 1), F32),
            pltpu.VMEM((rows, hw), F32),
        ],
    )
    return pl.pallas_call(
        functools.partial(_attn_s_body, n_pages_step=pps, n_heads=n_heads, head_dim=head_dim,
                          s_new=s_new),
        grid_spec=grid_spec,
        out_shape=jax.ShapeDtypeStruct((bd, s_new, d), F32),
        compiler_params=_cparams("parallel", "arbitrary"),
        name="attn_s",
    )(page_table, q, k_new.reshape(bd, s_new * n_heads, hw), v_new.reshape(bd, s_new * n_heads, hw),
      *lam_vecs, subln_g, *([ck] * pps), *([cv] * pps))


def _ffn0_body(x_ref, o_ref, wo_ref, g_ref, wg_ref, wu_ref, wd_ref, y_ref, *, ff_chunks):
    x1 = x_ref[...] + jnp.dot(o_ref[...].astype(BF16), wo_ref[...], preferred_element_type=F32)
    h = _rms(x1, g_ref[...]).astype(BF16)
    y = x1
    for lo, hi in ff_chunks:
        gate = jnp.dot(h, wg_ref[:, lo:hi], preferred_element_type=F32)
        up = jnp.dot(h, wu_ref[:, lo:hi], preferred_element_type=F32)
        a = (_silu(gate) * up).astype(BF16)
        y = y + jnp.dot(a, wd_ref[lo:hi, :], preferred_element_type=F32)
    y_ref[...] = y


def _ffn0(x, o, wo_bf, g, wg_bf, wu_bf, wd_bf):
    n, d = x.shape
    f = wg_bf.shape[1]
    tm = _tile(ROW_TILE_FFN, n)
    half = (f // 256) * 128
    ff_chunks = ((0, half), (half, f))
    row = lambda i: (i, 0)
    return pl.pallas_call(
        functools.partial(_ffn0_body, ff_chunks=ff_chunks),
        grid=(n // tm,),
        in_specs=[
            pl.BlockSpec((tm, d), row),
            pl.BlockSpec((tm, d), row),
            _const_spec((d, d)),
            _const_spec((1, d)),
            _const_spec((d, f)),
            _const_spec((d, f)),
            _const_spec((f, d)),
        ],
        out_specs=pl.BlockSpec((tm, d), row),
        out_shape=jax.ShapeDtypeStruct((n, d), F32),
        compiler_params=_cparams("parallel"),
        name="ffn0",
    )(x, o, wo_bf, g, wg_bf, wu_bf, wd_bf)


def _glu_pointwise(x, g_ref, w1_ref, b1_ref, d):
    h = _rms(x, g_ref[...]).astype(BF16)
    u = jnp.dot(h, w1_ref[...], preferred_element_type=F32) + b1_ref[...]
    return u[:, :d] * jax.nn.sigmoid(u[:, d:])


def _ln_swish_pointwise(c, lng_ref, lnb_ref, w2_ref, b2_ref):
    mu = jnp.mean(c, axis=-1, keepdims=True)
    xc = c - mu
    yln = xc * lax.rsqrt(jnp.mean(xc * xc, axis=-1, keepdims=True) + EPS) * lng_ref[...] + lnb_ref[...]
    return jnp.dot(_silu(yln).astype(BF16), w2_ref[...], preferred_element_type=F32) + b2_ref[...]


def _conv_p_body(x_ref, g_ref, w1_ref, b1_ref, wdw_ref, bdw_ref, lng_ref, lnb_ref, w2_ref, b2_ref,
                 y_ref, st_ref, u_sc, sh_sc, c_sc, *, tt, width):
    t = pl.program_id(1)
    d = x_ref.shape[-1]
    keep = width - 1

    @pl.when(t == 0)
    def _():
        u_sc[0:CONV_HALO, :] = jnp.zeros((CONV_HALO, d), F32)

    x = x_ref[...]
    u_sc[CONV_HALO:CONV_HALO + tt, :] = _glu_pointwise(x, g_ref, w1_ref, b1_ref, d)

    base = CONV_HALO - keep
    span = sh_sc.shape[1]
    for b in range(1, SUBLANES):
        sh_sc[b - 1, :, :] = u_sc[b:b + span, :]

    def chunk(r, carry):
        r0 = pl.multiple_of(r * CONV_ROWS, CONV_ROWS)
        acc = jnp.zeros((CONV_ROWS, d), F32)
        for j in range(width):
            a, b = divmod(base + j, SUBLANES)
            start = pl.multiple_of(r0 + SUBLANES * a, SUBLANES)
            if b == 0:
                tap = u_sc[pl.ds(start, CONV_ROWS), :]
            else:
                tap = sh_sc[b - 1, pl.ds(start, CONV_ROWS), :]
            acc = acc + wdw_ref[j:j + 1, :] * tap
        c_sc[pl.ds(r0, CONV_ROWS), :] = acc
        return carry

    lax.fori_loop(0, tt // CONV_ROWS, chunk, 0)

    c = c_sc[...] + bdw_ref[...]
    y_ref[...] = x + _ln_swish_pointwise(c, lng_ref, lnb_ref, w2_ref, b2_ref)
    st_ref[...] = u_sc[CONV_HALO + tt - keep:CONV_HALO + tt, :]
    u_sc[0:CONV_HALO, :] = u_sc[tt:tt + CONV_HALO, :]


def _conv_prompt(x, batch, seq, g, w1_bf, b1, wdw, bdw, lng, lnb, w2_bf, b2):
    n, d = x.shape
    width = wdw.shape[0]
    tt = _tile(CONV_TILE, seq)
    nt = seq // tt
    row = lambda b, t: (b * nt + t, 0)
    y, st = pl.pallas_call(
        functools.partial(_conv_p_body, tt=tt, width=width),
        grid=(batch, nt),
        in_specs=[
            pl.BlockSpec((tt, d), row),
            _const_spec((1, d)),
            _const_spec((d, 2 * d)),
            _const_spec((1, 2 * d)),
            _const_spec((width, d)),
            _const_spec((1, d)),
            _const_spec((1, d)),
            _const_spec((1, d)),
            _const_spec((d, d)),
            _const_spec((1, d)),
        ],
        out_specs=[
            pl.BlockSpec((tt, d), row),
            pl.BlockSpec((None, width - 1, d), lambda b, t: (b, 0, 0)),
        ],
        out_shape=[
            jax.ShapeDtypeStruct((n, d), F32),
            jax.ShapeDtypeStruct((batch, width - 1, d), F32),
        ],
        scratch_shapes=[
            pltpu.VMEM((CONV_HALO + tt, d), F32),
            pltpu.VMEM((SUBLANES - 1, CONV_HALO + tt - SUBLANES, d), F32),
            pltpu.VMEM((tt, d), F32),
        ],
        compiler_params=_cparams("parallel", "arbitrary"),
        name="conv_p",
    )(x, g, w1_bf, b1, wdw, bdw, lng, lnb, w2_bf, b2)
    return y, st


def _conv_s_body(x_ref, st_in_ref, g_ref, w1_ref, b1_ref, wdw_ref, bdw_ref, lng_ref, lnb_ref,
                 w2_ref, b2_ref, y_ref, st_out_ref, ext_sc, c_sc, *, s_new, width, n_seq):
    d = x_ref.shape[-1]
    keep = width - 1
    x = x_ref[...]
    ug = _glu_pointwise(x, g_ref, w1_ref, b1_ref, d)
    ext_sc[:, 0:keep, :] = st_in_ref[...]
    for s in range(n_seq):
        ext_sc[s, keep:keep + s_new, :] = ug[s * s_new:(s + 1) * s_new, :]
    acc = jnp.zeros((n_seq, s_new, d), F32)
    for j in range(width):
        acc = acc + wdw_ref[j:j + 1, :][None] * ext_sc[:, j:j + s_new, :]
    for s in range(n_seq):
        c_sc[s * s_new:(s + 1) * s_new, :] = acc[s]
    st_out_ref[...] = ext_sc[:, s_new:s_new + keep, :]
    c = c_sc[...] + bdw_ref[...]
    y_ref[...] = x + _ln_swish_pointwise(c, lng_ref, lnb_ref, w2_ref, b2_ref)


def _conv_sample(x, state, s_new, g, w1_bf, b1, wdw, bdw, lng, lnb, w2_bf, b2):
    n, d = x.shape
    bd = state.shape[0]
    width = wdw.shape[0]
    keep = width - 1
    ns = _tile(CONV_SEQ_BLOCK, bd)
    y, st = pl.pallas_call(
        functools.partial(_conv_s_body, s_new=s_new, width=width, n_seq=ns),
        grid=(bd // ns,),
        in_specs=[
            pl.BlockSpec((ns * s_new, d), lambda i: (i, 0)),
            pl.BlockSpec((ns, keep, d), lambda i: (i, 0, 0)),
            _const_spec((1, d)),
            _const_spec((d, 2 * d)),
            _const_spec((1, 2 * d)),
            _const_spec((width, d)),
            _const_spec((1, d)),
            _const_spec((1, d)),
            _const_spec((1, d)),
            _const_spec((d, d)),
            _const_spec((1, d)),
        ],
        out_specs=[
            pl.BlockSpec((ns * s_new, d), lambda i: (i, 0)),
            pl.BlockSpec((ns, keep, d), lambda i: (i, 0, 0)),
        ],
        out_shape=[
            jax.ShapeDtypeStruct((n, d), F32),
            jax.ShapeDtypeStruct((bd, keep, d), F32),
        ],
        scratch_shapes=[
            pltpu.VMEM((ns, keep + s_new, d), F32),
            pltpu.VMEM((ns * s_new, d), F32),
        ],
        compiler_params=_cparams("parallel"),
        name="conv_s",
    )(x, state, g, w1_bf, b1, wdw, bdw, lng, lnb, w2_bf, b2)
    return y, st


def _router_body(x_ref, g_ref, wrt_ref, h_ref, idx_ref, gate_ref):
    h = _rms(x_ref[...], g_ref[...])
    h_ref[...] = h.astype(BF16)
    logits = lax.dot_general(wrt_ref[...], h, (((1,), (1,)), ((), ())),
                             precision=lax.Precision.HIGHEST,
                             preferred_element_type=F32)
    n_e = float(logits.shape[0])
    eidx = lax.broadcasted_iota(jnp.int32, logits.shape, 0).astype(F32)
    v1 = jnp.max(logits, axis=0, keepdims=True)
    i1 = jnp.min(jnp.where(logits == v1, eidx, n_e), axis=0, keepdims=True)
    rest = jnp.where(eidx == i1, -jnp.inf, logits)
    v2 = jnp.max(rest, axis=0, keepdims=True)
    i2 = jnp.min(jnp.where(rest == v2, eidx, n_e), axis=0, keepdims=True)
    e = jnp.exp(v2 - v1)
    g1 = 1.0 / (1.0 + e)
    idx_ref[0:1, :] = i1.astype(jnp.int32)
    idx_ref[1:2, :] = i2.astype(jnp.int32)
    gate_ref[0:1, :] = g1
    gate_ref[1:2, :] = e * g1


def _router(x, g, w_router_t):
    n, d = x.shape
    n_e = w_router_t.shape[0]
    tm = _tile(ROUTER_TILE, n)
    return pl.pallas_call(
        _router_body,
        grid=(n // tm,),
        in_specs=[
            pl.BlockSpec((tm, d), lambda i: (i, 0)),
            _const_spec((1, d)),
            _const_spec((n_e, d)),
        ],
        out_specs=[
            pl.BlockSpec((tm, d), lambda i: (i, 0)),
            pl.BlockSpec((TOP_K, tm), lambda i: (0, i)),
            pl.BlockSpec((TOP_K, tm), lambda i: (0, i)),
        ],
        out_shape=[
            jax.ShapeDtypeStruct((n, d), BF16),
            jax.ShapeDtypeStruct((TOP_K, n), jnp.int32),
            jax.ShapeDtypeStruct((TOP_K, n), F32),
        ],
        compiler_params=_cparams("parallel"),
        name="router",
    )(x, g, w_router_t)


def _moe_body(te_ref, tv_ref, ts_ref, x_ref, gt_ref, wg_ref, wu_ref, wd_ref, y_ref, acc_sc):
    t = pl.program_id(0)
    c = pl.program_id(1)

    @pl.when(tv_ref[t] == 1)
    def _():
        @pl.when(c == 0)
        def _():
            acc_sc[...] = jnp.zeros(acc_sc.shape, F32)

        x = x_ref[...]
        gate = jnp.dot(x, wg_ref[...].astype(BF16), preferred_element_type=F32)
        up = jnp.dot(x, wu_ref[...].astype(BF16), preferred_element_type=F32)
        a = (_silu(gate) * up).astype(BF16)
        acc_sc[...] += jnp.dot(a, wd_ref[...].astype(BF16), preferred_element_type=F32)

        @pl.when(c == pl.num_programs(1) - 1)
        def _():
            y_ref[...] = acc_sc[...] * gt_ref[...]


def _moe(xs, gate_slot, tile_expert, tile_valid, tile_src, w_gate_e, w_up_e, w_down_e, tm):
    p_rows, d = xs.shape
    n_e, _, f = w_gate_e.shape
    fc = _tile(MOE_FF_CHUNK, f)
    nc = f // fc
    n_tiles = p_rows // tm

    def chunk_of(t, c, tv):
        return jnp.where(tv[t] == 1, c, nc - 1)

    grid_spec = pltpu.PrefetchScalarGridSpec(
        num_scalar_prefetch=3,
        grid=(n_tiles, nc),
        in_specs=[
            pl.BlockSpec((tm, d), lambda t, c, te, tv, ts: (ts[t], 0)),
            pl.BlockSpec((tm, 1), lambda t, c, te, tv, ts: (ts[t], 0)),
            pl.BlockSpec((None, d, fc), lambda t, c, te, tv, ts: (te[t], 0, chunk_of(t, c, tv))),
            pl.BlockSpec((None, d, fc), lambda t, c, te, tv, ts: (te[t], 0, chunk_of(t, c, tv))),
            pl.BlockSpec((None, fc, d), lambda t, c, te, tv, ts: (te[t], chunk_of(t, c, tv), 0)),
        ],
        out_specs=pl.BlockSpec((tm, d), lambda t, c, te, tv, ts: (ts[t], 0)),
        scratch_shapes=[pltpu.VMEM((tm, d), F32)],
    )
    return pl.pallas_call(
        _moe_body,
        grid_spec=grid_spec,
        out_shape=jax.ShapeDtypeStruct((p_rows, d), F32),
        compiler_params=_cparams("arbitrary", "arbitrary"),
        name="moe",
    )(tile_expert, tile_valid, tile_src, xs, gate_slot, w_gate_e, w_up_e, w_down_e)


def _moe_layer(x, g, w_router, w_gate_e, w_up_e, w_down_e):
    n, d = x.shape
    n_e = w_router.shape[1]
    tm = min(MOE_TILE, n)
    h_bf, idx, gates = _router(x, g, w_router.T)

    e_flat = idx.reshape(-1)
    onehot = (e_flat[:, None] == jnp.arange(n_e, dtype=jnp.int32)[None, :]).astype(jnp.int32)
    csum = jnp.cumsum(onehot, axis=0)
    rank = jnp.sum(onehot * csum, axis=1) - 1
    counts = csum[-1]
    padded = ((counts + tm - 1) // tm) * tm
    ends = jnp.cumsum(padded)
    dest = (ends - padded)[e_flat] + rank
    n_tiles = (TOP_K * n) // tm + n_e
    p_rows = n_tiles * tm
    tile_start = jnp.arange(n_tiles, dtype=jnp.int32) * tm
    tile_valid = (tile_start < ends[-1]).astype(jnp.int32)
    last_valid = jnp.maximum(jnp.sum(tile_valid) - 1, 0)
    tile_src = jnp.minimum(jnp.arange(n_tiles, dtype=jnp.int32), last_valid)
    tile_expert = jnp.minimum(
        jnp.searchsorted(ends, tile_src * tm, side="right"), n_e - 1).astype(jnp.int32)
    token = jnp.tile(jnp.arange(n, dtype=jnp.int32), TOP_K)
    src_token = jnp.zeros((p_rows,), jnp.int32).at[dest].set(token)
    gate_slot = jnp.zeros((p_rows,), F32).at[dest].set(gates.reshape(-1))

    xs = jnp.take(h_bf, src_token, axis=0)
    ys = _moe(xs, gate_slot[:, None], tile_expert, tile_valid, tile_src,
              w_gate_e, w_up_e, w_down_e, tm)
    y0 = jnp.take(ys, dest[:n], axis=0)
    y1 = jnp.take(ys, dest[n:], axis=0)
    return x + (y0 + y1)


def kernel(x_prompt, x_sample, cache_k, cache_v, state_conv, page_table, norm_attn, w_qkv, q_norm, k_norm, lambda_q1, lambda_k1, lambda_q2, lambda_k2, subln_g, w_o, norm_ffn0, w_gate0, w_up0, w_down0, norm_conv, w_pw1, b_pw1, w_dw, b_dw, ln_g, ln_b, w_pw2, b_pw2, norm_ffn1, w_router, w_gate_e, w_up_e, w_down_e):
    batch, seq, d = x_prompt.shape
    bd, s_new, _ = x_sample.shape
    n_heads = cache_k.shape[2]
    head_dim = q_norm.shape[0]
    hw = 2 * head_dim
    n_p = batch * seq
    n_s = bd * s_new

    row = lambda v: v.reshape(1, -1)
    bf = lambda w: w.astype(BF16)
    xp = x_prompt.reshape(n_p, d)
    xs = x_sample.reshape(n_s, d)
    grp = jnp.kron(jnp.eye(d // head_dim, dtype=F32), jnp.ones((head_dim, head_dim), F32)).astype(BF16)
    qn_t = row(jnp.tile(q_norm, d // head_dim))
    kn_t = row(jnp.tile(k_norm, d // head_dim))
    lam_vecs = (row(lambda_q1), row(lambda_k1), row(lambda_q2), row(lambda_k2))
    sg = row(subln_g)
    w_qkv_bf = bf(w_qkv)

    qk_scale = head_dim ** -0.5
    qp, kp, vp, kpb, vpb = _qkv(xp, row(norm_attn), w_qkv_bf, qn_t, kn_t, grp, head_dim, BF16,
                                qk_scale * math.log2(math.e))
    qs, ks, vs, _, _ = _qkv(xs, row(norm_attn), w_qkv_bf, qn_t, kn_t, grp, head_dim, F32, qk_scale)
    op = _attn_prompt(qp, kpb, vpb, lam_vecs, sg, batch, seq, n_heads, head_dim)
    osm = _attn_sample(qs.reshape(bd, s_new, d), ks.reshape(bd, s_new, d), vs.reshape(bd, s_new, d),
                       cache_k, cache_v, page_table, lam_vecs, sg, n_heads, head_dim)

    ffn0_w = (bf(w_o), row(norm_ffn0), bf(w_gate0), bf(w_up0), bf(w_down0))
    xp = _ffn0(xp, op, *ffn0_w)
    xs = _ffn0(xs, osm.reshape(n_s, d), *ffn0_w)

    conv_w = (row(norm_conv), bf(w_pw1), row(b_pw1), w_dw, row(b_dw), row(ln_g), row(ln_b),
              bf(w_pw2), row(b_pw2))
    xp, conv_prompt = _conv_prompt(xp, batch, seq, *conv_w)
    xs, conv_sample = _conv_sample(xs, state_conv, s_new, *conv_w)

    x_all = _moe_layer(jnp.concatenate([xp, xs], axis=0), row(norm_ffn1), w_router,
                       w_gate_e, w_up_e, w_down_e)

    return (x_all[:n_p].reshape(batch, seq, d), x_all[n_p:].reshape(bd, s_new, d),
            kp.reshape(batch, seq, n_heads, hw), vp.reshape(batch, seq, n_heads, hw),
            ks.reshape(bd, s_new, n_heads, hw), vs.reshape(bd, s_new, n_heads, hw),
            conv_prompt, conv_sample)
```

```python
import functools
import math

import jax
import jax.numpy as jnp
from jax import lax
from jax.experimental import pallas as pl
from jax.experimental.pallas import tpu as pltpu

F32 = jnp.float32
BF16 = jnp.bfloat16

EPS = 1e-6
LAMBDA_INIT = 0.8 - 0.6 * math.exp(-0.3 * 0)
TOP_K = 2
NEG = -1e30
SUBLANES = 8

VMEM_LIMIT_BYTES = 56 * 1024 * 1024

ROW_TILE_QKV = 256
ROW_TILE_FFN = 512
ATTN_TILE = 512
ATTN_HEADS_PER_STEP = 2
ATTN_ROW_CHUNK = 256
PAGES_PER_STEP = 8
CONV_TILE = 512
CONV_HALO = 32
CONV_ROWS = 16
CONV_SEQ_BLOCK = 32
ROUTER_TILE = 512
MOE_TILE = 1024
MOE_FF_CHUNK = 512
DISPATCH_TILE = 256
GATE_LANES = 128


def _cparams(*sem):
    return pltpu.CompilerParams(dimension_semantics=sem, vmem_limit_bytes=VMEM_LIMIT_BYTES)


def _const_spec(shape):
    nd = len(shape)
    return pl.BlockSpec(shape, lambda *_: (0,) * nd, pipeline_mode=pl.Buffered(1))


def _rms(x, g):
    return x * lax.rsqrt(jnp.mean(x * x, axis=-1, keepdims=True) + EPS) * g


def _silu(x):
    return x * jax.nn.sigmoid(x)


def _tile(limit, n):
    t = min(limit, n)
    assert n % t == 0, (n, t)
    return t


def _div_pow2(x, c):
    assert c > 0 and c & (c - 1) == 0, c
    return lax.shift_right_logical(x, c.bit_length() - 1)


def _mod_pow2(x, c):
    assert c > 0 and c & (c - 1) == 0, c
    return lax.bitwise_and(x, c - 1)


def _qkv_body(x_ref, g_ref, w_ref, qn_ref, kn_ref, grp_ref, q_out, k_out, v_out, kb_out, vb_out,
              *, d_model, head_dim, q_scale):
    x = x_ref[...]
    h = _rms(x, g_ref[...]).astype(BF16)
    qkv = jnp.dot(h, w_ref[...], preferred_element_type=F32)
    q = qkv[:, :d_model]
    k = qkv[:, d_model:2 * d_model]
    v = qkv[:, 2 * d_model:]
    grp = grp_ref[...]

    def head_norm(t, gain):
        ssq = jnp.dot((t * t).astype(BF16), grp, preferred_element_type=F32)
        return t * lax.rsqrt(ssq * (1.0 / head_dim) + EPS) * gain

    qn = head_norm(q, qn_ref[...]) * q_scale
    kn = head_norm(k, kn_ref[...])
    q_out[...] = qn.astype(q_out.dtype)
    k_out[...] = kn
    v_out[...] = v
    kb_out[...] = kn.astype(BF16)
    vb_out[...] = v.astype(BF16)


def _qkv(x, g, w_bf, qn_t, kn_t, grp, head_dim, q_dtype, q_scale):
    n, d = x.shape
    tm = _tile(ROW_TILE_QKV, n)
    row = lambda i: (i, 0)
    out = pl.pallas_call(
        functools.partial(_qkv_body, d_model=d, head_dim=head_dim, q_scale=q_scale),
        grid=(n // tm,),
        in_specs=[
            pl.BlockSpec((tm, d), row),
            _const_spec((1, d)),
            _const_spec((d, 3 * d)),
            _const_spec((1, d)),
            _const_spec((1, d)),
            _const_spec((d, d)),
        ],
        out_specs=[pl.BlockSpec((tm, d), row)] * 5,
        out_shape=[
            jax.ShapeDtypeStruct((n, d), q_dtype),
            jax.ShapeDtypeStruct((n, d), F32),
            jax.ShapeDtypeStruct((n, d), F32),
            jax.ShapeDtypeStruct((n, d), BF16),
            jax.ShapeDtypeStruct((n, d), BF16),
        ],
        compiler_params=_cparams("parallel"),
        name="qkv",
    )(x, g, w_bf, qn_t, kn_t, grp)
    return out


def _lambda(lq1, lk1, lq2, lk2):
    a = jnp.sum(lq1[...] * lk1[...], axis=-1, keepdims=True)
    b = jnp.sum(lq2[...] * lk2[...], axis=-1, keepdims=True)
    return jnp.exp(a) - jnp.exp(b) + LAMBDA_INIT


def _attn_p_body(qi_tab, ki_tab, q_ref, k_ref, v_ref, lq1, lk1, lq2, lk2, sg_ref, o_ref,
                 qq_sc, m_sc, l_sc, acc_sc, *, tq, head_dim, heads):
    step = pl.program_id(2)
    qi = qi_tab[step]
    ki = ki_tab[step]
    hw = 2 * head_dim
    rc = _tile(ATTN_ROW_CHUNK, tq)

    @pl.when(ki == 0)
    def _():
        for h in range(heads):
            q = q_ref[:, h * hw:(h + 1) * hw]
            lane = lax.broadcasted_iota(jnp.int32, q.shape, 1)
            zero = jnp.zeros_like(q)
            qq_sc[h, 0:tq, :] = jnp.where(lane < head_dim, q, zero)
            qq_sc[h, tq:2 * tq, :] = jnp.where(lane >= head_dim, q, zero)
        m_sc[...] = jnp.full(m_sc.shape, NEG, F32)
        l_sc[...] = jnp.zeros(l_sc.shape, F32)
        acc_sc[...] = jnp.zeros(acc_sc.shape, F32)

    def update(diagonal):
        for h in range(heads):
            lanes = slice(h * hw, (h + 1) * hw)
            for r0 in range(0, 2 * tq, rc):
                q0 = r0 % tq
                nk = q0 + rc if diagonal else tq
                rows = slice(r0, r0 + rc)
                s = lax.dot_general(qq_sc[h, rows, :], k_ref[0:nk, lanes], (((1,), (1,)), ((), ())),
                                    preferred_element_type=F32)
                if diagonal:
                    row = lax.broadcasted_iota(jnp.int32, s.shape, 0) + q0
                    col = lax.broadcasted_iota(jnp.int32, s.shape, 1)
                    s = jnp.where(col <= row, s, NEG)
                m_prev = m_sc[h, rows, :]
                m_new = jnp.maximum(m_prev, jnp.max(s, axis=-1, keepdims=True))
                alpha = jnp.exp2(m_prev - m_new)
                p = jnp.exp2(s - jnp.concatenate([m_new] * (nk // hw), axis=1)).astype(BF16)
                v1 = jnp.concatenate([v_ref[0:nk, lanes], jnp.ones((nk, hw), BF16)], axis=1)
                pv = jnp.dot(p, v1, preferred_element_type=F32)
                l_sc[h, rows, :] = alpha * l_sc[h, rows, :] + pv[:, hw:]
                acc_sc[h, rows, :] = alpha * acc_sc[h, rows, :] + pv[:, :hw]
                m_sc[h, rows, :] = m_new

    @pl.when(ki < qi)
    def _():
        update(False)

    @pl.when(ki == qi)
    def _():
        update(True)
        lam = _lambda(lq1, lk1, lq2, lk2)
        for h in range(heads):
            o1 = acc_sc[h, 0:tq, :] / l_sc[h, 0:tq, :]
            o2 = acc_sc[h, tq:2 * tq, :] / l_sc[h, tq:2 * tq, :]
            o = o1 - lam * o2
            o_ref[:, h * hw:(h + 1) * hw] = (_rms(o, sg_ref[...]) * (1.0 - LAMBDA_INIT)).astype(o_ref.dtype)


def _attn_prompt(q_bf, k_bf, v_bf, lam_vecs, subln_g, batch, seq, n_heads, head_dim):
    n, d = q_bf.shape
    hw = 2 * head_dim
    tq = _tile(ATTN_TILE, seq)
    nq = seq // tq
    heads = _tile(ATTN_HEADS_PER_STEP, n_heads)
    pairs = [(i, j) for i in range(nq) for j in range(i + 1)]
    qi_tab = jnp.asarray([p[0] for p in pairs], jnp.int32)
    ki_tab = jnp.asarray([p[1] for p in pairs], jnp.int32)
    q_map = lambda b, h, s, qt, kt: (b * nq + qt[s], h)
    kv_map = lambda b, h, s, qt, kt: (b * nq + kt[s], h)
    vec = pl.BlockSpec((1, head_dim), lambda b, h, s, qt, kt: (0, 0))
    grid_spec = pltpu.PrefetchScalarGridSpec(
        num_scalar_prefetch=2,
        grid=(batch, n_heads // heads, len(pairs)),
        in_specs=[
            pl.BlockSpec((tq, heads * hw), q_map),
            pl.BlockSpec((tq, heads * hw), kv_map),
            pl.BlockSpec((tq, heads * hw), kv_map),
            vec, vec, vec, vec,
            pl.BlockSpec((1, hw), lambda b, h, s, qt, kt: (0, 0)),
        ],
        out_specs=pl.BlockSpec((tq, heads * hw), q_map),
        scratch_shapes=[
            pltpu.VMEM((heads, 2 * tq, hw), BF16),
            pltpu.VMEM((heads, 2 * tq, hw), F32),
            pltpu.VMEM((heads, 2 * tq, hw), F32),
            pltpu.VMEM((heads, 2 * tq, hw), F32),
        ],
    )
    return pl.pallas_call(
        functools.partial(_attn_p_body, tq=tq, head_dim=head_dim, heads=heads),
        grid_spec=grid_spec,
        out_shape=jax.ShapeDtypeStruct((n, d), BF16),
        compiler_params=_cparams("parallel", "parallel", "arbitrary"),
        name="attn_p",
    )(qi_tab, ki_tab, q_bf, k_bf, v_bf, *lam_vecs, subln_g)


def _attn_s_body(pt_ref, q_ref, kn_ref, vn_ref, lq1, lk1, lq2, lk2, sg_ref, *rest,
                 n_pages_step, n_heads, head_dim, s_new):
    k_refs = rest[:n_pages_step]
    v_refs = rest[n_pages_step:2 * n_pages_step]
    o_ref = rest[2 * n_pages_step]
    qm_sc, bias_sc, m_sc, l_sc, acc_sc = rest[2 * n_pages_step + 1:]
    j = pl.program_id(1)
    rows, hw = acc_sc.shape
    per_head = 2 * s_new
    nt = (((1,), (1,)), ((), ()))

    @pl.when(j == 0)
    def _():
        q = q_ref[...]
        r8 = lax.broadcasted_iota(jnp.int32, (per_head, hw), 0)
        l8 = lax.broadcasted_iota(jnp.int32, (per_head, hw), 1)
        own_map = _div_pow2(l8, head_dim) == _div_pow2(r8, s_new)
        blocks = []
        for h in range(n_heads):
            qh = q[:, h * hw:(h + 1) * hw]
            blk = jnp.zeros((per_head, hw), F32)
            for t in range(s_new):
                blk = jnp.where(own_map & (_mod_pow2(r8, s_new) == t), qh[t:t + 1, :], blk)
            blocks.append(blk)
        qm = jnp.concatenate(blocks, axis=0).astype(BF16)
        qm_sc[...] = qm
        ridx = lax.broadcasted_iota(jnp.int32, bias_sc.shape, 0)
        col = lax.broadcasted_iota(jnp.int32, bias_sc.shape, 1)
        bias_sc[...] = jnp.where(_mod_pow2(col, n_heads) == _div_pow2(ridx, per_head), 0.0, NEG)
        n_new = kn_ref.shape[0]
        s = lax.dot_general(qm, kn_ref[...].astype(BF16), nt, preferred_element_type=F32)
        ridx = lax.broadcasted_iota(jnp.int32, (rows, n_new), 0)
        col = lax.broadcasted_iota(jnp.int32, (rows, n_new), 1)
        ok = (_mod_pow2(col, n_heads) == _div_pow2(ridx, per_head)) & (
            _div_pow2(col, n_heads) <= _mod_pow2(ridx, s_new))
        s = jnp.where(ok, s, NEG)
        m = jnp.max(s, axis=-1, keepdims=True)
        p = jnp.exp(s - m)
        m_sc[...] = m
        l_sc[...] = jnp.sum(p, axis=-1, keepdims=True)
        acc_sc[...] = jnp.dot(p.astype(BF16), vn_ref[...].astype(BF16), preferred_element_type=F32)

    qm = qm_sc[...]
    bias = bias_sc[...]
    scores = [lax.dot_general(qm, k_refs[i][...].astype(BF16), nt, preferred_element_type=F32) + bias
              for i in range(n_pages_step)]
    m_prev = m_sc[...]
    m_new = m_prev
    for s in scores:
        m_new = jnp.maximum(m_new, jnp.max(s, axis=-1, keepdims=True))
    alpha = jnp.exp(m_prev - m_new)
    l = alpha * l_sc[...]
    acc = alpha * acc_sc[...]
    for i, s in enumerate(scores):
        p = jnp.exp(s - m_new)
        l = l + jnp.sum(p, axis=-1, keepdims=True)
        acc = acc + jnp.dot(p.astype(BF16), v_refs[i][...].astype(BF16), preferred_element_type=F32)
    m_sc[...] = m_new
    l_sc[...] = l
    acc_sc[...] = acc

    @pl.when(j == pl.num_programs(1) - 1)
    def _():
        accn = acc / l
        lam = _lambda(lq1, lk1, lq2, lk2)
        sg = sg_ref[...]
        for h in range(n_heads):
            blk = accn[h * per_head:(h + 1) * per_head]
            oh = blk[0:s_new] - lam * blk[s_new:per_head]
            o_ref[:, h * hw:(h + 1) * hw] = _rms(oh, sg) * (1.0 - LAMBDA_INIT)


def _attn_sample(q, k_new, v_new, cache_k, cache_v, page_table, lam_vecs, subln_g, n_heads, head_dim):
    bd, s_new, d = q.shape
    n_phys, page, _, hw = cache_k.shape
    prow = page * n_heads
    ck = cache_k.reshape(n_phys, prow, hw)
    cv = cache_v.reshape(n_phys, prow, hw)
    n_pages = page_table.shape[1]
    pps = _tile(PAGES_PER_STEP, n_pages)
    rows = n_heads * 2 * s_new
    seq_spec = pl.BlockSpec((None, s_new, d), lambda b, j, pt: (b, 0, 0))
    new_spec = pl.BlockSpec((None, s_new * n_heads, hw), lambda b, j, pt: (b, 0, 0))
    vec = pl.BlockSpec((1, head_dim), lambda b, j, pt: (0, 0))

    def page_spec(i):
        return pl.BlockSpec((None, prow, hw), lambda b, j, pt: (pt[b, j * pps + i], 0, 0))

    grid_spec = pltpu.PrefetchScalarGridSpec(
        num_scalar_prefetch=1,
        grid=(bd, n_pages // pps),
        in_specs=[seq_spec, new_spec, new_spec, vec, vec, vec, vec,
                  pl.BlockSpec((1, hw), lambda b, j, pt: (0, 0))]
                 + [page_spec(i) for i in range(pps)] + [page_spec(i) for i in range(pps)],
        out_specs=seq_spec,
        scratch_shapes=[
            pltpu.VMEM((rows, hw), BF16),
            pltpu.VMEM((rows, prow), F32),
            pltpu.VMEM((rows, 1), F32),
            pltpu.VMEM((rows, 1), F32),
            pltpu.VMEM((rows, hw), F32),
        ],
    )
    return pl.pallas_call(
        functools.partial(_attn_s_body, n_pages_step=pps, n_heads=n_heads, head_dim=head_dim,
                          s_new=s_new),
        grid_spec=grid_spec,
        out_shape=jax.ShapeDtypeStruct((bd, s_new, d), F32),
        compiler_params=_cparams("parallel", "arbitrary"),
        name="attn_s",
    )(page_table, q, k_new.reshape(bd, s_new * n_heads, hw), v_new.reshape(bd, s_new * n_heads, hw),
      *lam_vecs, subln_g, *([ck] * pps), *([cv] * pps))


def _ffn0_body(x_ref, o_ref, wo_ref, g_ref, wg_ref, wu_ref, wd_ref, y_ref, *, ff_chunks):
    x1 = x_ref[...] + jnp.dot(o_ref[...].astype(BF16), wo_ref[...], preferred_element_type=F32)
    h = _rms(x1, g_ref[...]).astype(BF16)
    y = x1
    for lo, hi in ff_chunks:
        gate = jnp.dot(h, wg_ref[:, lo:hi], preferred_element_type=F32)
        up = jnp.dot(h, wu_ref[:, lo:hi], preferred_element_type=F32)
        a = (_silu(gate) * up).astype(BF16)
        y = y + jnp.dot(a, wd_ref[lo:hi, :], preferred_element_type=F32)
    y_ref[...] = y


def _ffn0(x, o, wo_bf, g, wg_bf, wu_bf, wd_bf):
    n, d = x.shape
    f = wg_bf.shape[1]
    tm = _tile(ROW_TILE_FFN, n)
    half = (f // 256) * 128
    ff_chunks = ((0, half), (half, f))
    row = lambda i: (i, 0)
    return pl.pallas_call(
        functools.partial(_ffn0_body, ff_chunks=ff_chunks),
        grid=(n // tm,),
        in_specs=[
            pl.BlockSpec((tm, d), row),
            pl.BlockSpec((tm, d), row),
            _const_spec((d, d)),
            _const_spec((1, d)),
            _const_spec((d, f)),
            _const_spec((d, f)),
            _const_spec((f, d)),
        ],
        out_specs=pl.BlockSpec((tm, d), row),
        out_shape=jax.ShapeDtypeStruct((n, d), F32),
        compiler_params=_cparams("parallel"),
        name="ffn0",
    )(x, o, wo_bf, g, wg_bf, wu_bf, wd_bf)


def _glu_pointwise(x, g_ref, w1_ref, b1_ref, d):
    h = _rms(x, g_ref[...]).astype(BF16)
    u = jnp.dot(h, w1_ref[...], preferred_element_type=F32) + b1_ref[...]
    return u[:, :d] * jax.nn.sigmoid(u[:, d:])


def _ln_swish_pointwise(c, lng_ref, lnb_ref, w2_ref, b2_ref):
    mu = jnp.mean(c, axis=-1, keepdims=True)
    xc = c - mu
    yln = xc * lax.rsqrt(jnp.mean(xc * xc, axis=-1, keepdims=True) + EPS) * lng_ref[...] + lnb_ref[...]
    return jnp.dot(_silu(yln).astype(BF16), w2_ref[...], preferred_element_type=F32) + b2_ref[...]


def _conv_p_body(x_ref, g_ref, w1_ref, b1_ref, wdw_ref, bdw_ref, lng_ref, lnb_ref, w2_ref, b2_ref,
                 y_ref, st_ref, u_sc, sh_sc, c_sc, *, tt, width):
    t = pl.program_id(1)
    d = x_ref.shape[-1]
    keep = width - 1

    @pl.when(t == 0)
    def _():
        u_sc[0:CONV_HALO, :] = jnp.zeros((CONV_HALO, d), F32)

    x = x_ref[...]
    u_sc[CONV_HALO:CONV_HALO + tt, :] = _glu_pointwise(x, g_ref, w1_ref, b1_ref, d)

    base = CONV_HALO - keep
    span = sh_sc.shape[1]
    for b in range(1, SUBLANES):
        sh_sc[b - 1, :, :] = u_sc[b:b + span, :]

    def chunk(r, carry):
        r0 = pl.multiple_of(r * CONV_ROWS, CONV_ROWS)
        acc = jnp.zeros((CONV_ROWS, d), F32)
        for j in range(width):
            a, b = divmod(base + j, SUBLANES)
            start = pl.multiple_of(r0 + SUBLANES * a, SUBLANES)
            if b == 0:
                tap = u_sc[pl.ds(start, CONV_ROWS), :]
            else:
                tap = sh_sc[b - 1, pl.ds(start, CONV_ROWS), :]
            acc = acc + wdw_ref[j:j + 1, :] * tap
        c_sc[pl.ds(r0, CONV_ROWS), :] = acc
        return carry

    lax.fori_loop(0, tt // CONV_ROWS, chunk, 0)

    c = c_sc[...] + bdw_ref[...]
    y_ref[...] = x + _ln_swish_pointwise(c, lng_ref, lnb_ref, w2_ref, b2_ref)
    st_ref[...] = u_sc[CONV_HALO + tt - keep:CONV_HALO + tt, :]
    u_sc[0:CONV_HALO, :] = u_sc[tt:tt + CONV_HALO, :]


def _conv_prompt(x, batch, seq, g, w1_bf, b1, wdw, bdw, lng, lnb, w2_bf, b2):
    n, d = x.shape
    width = wdw.shape[0]
    tt = _tile(CONV_TILE, seq)
    nt = seq // tt
    row = lambda b, t: (b * nt + t, 0)
    y, st = pl.pallas_call(
        functools.partial(_conv_p_body, tt=tt, width=width),
        grid=(batch, nt),
        in_specs=[
            pl.BlockSpec((tt, d), row),
            _const_spec((1, d)),
            _const_spec((d, 2 * d)),
            _const_spec((1, 2 * d)),
            _const_spec((width, d)),
            _const_spec((1, d)),
            _const_spec((1, d)),
            _const_spec((1, d)),
            _const_spec((d, d)),
            _const_spec((1, d)),
        ],
        out_specs=[
            pl.BlockSpec((tt, d), row),
            pl.BlockSpec((None, width - 1, d), lambda b, t: (b, 0, 0)),
        ],
        out_shape=[
            jax.ShapeDtypeStruct((n, d), F32),
            jax.ShapeDtypeStruct((batch, width - 1, d), F32),
        ],
        scratch_shapes=[
            pltpu.VMEM((CONV_HALO + tt, d), F32),
            pltpu.VMEM((SUBLANES - 1, CONV_HALO + tt - SUBLANES, d), F32),
            pltpu.VMEM((tt, d), F32),
        ],
        compiler_params=_cparams("parallel", "arbitrary"),
        name="conv_p",
    )(x, g, w1_bf, b1, wdw, bdw, lng, lnb, w2_bf, b2)
    return y, st


def _conv_s_body(x_ref, st_in_ref, g_ref, w1_ref, b1_ref, wdw_ref, bdw_ref, lng_ref, lnb_ref,
                 w2_ref, b2_ref, y_ref, st_out_ref, ext_sc, c_sc, *, s_new, width, n_seq):
    d = x_ref.shape[-1]
    keep = width - 1
    x = x_ref[...]
    ug = _glu_pointwise(x, g_ref, w1_ref, b1_ref, d)
    ext_sc[:, 0:keep, :] = st_in_ref[...]
    for s in range(n_seq):
        ext_sc[s, keep:keep + s_new, :] = ug[s * s_new:(s + 1) * s_new, :]
    acc = jnp.zeros((n_seq, s_new, d), F32)
    for j in range(width):
        acc = acc + wdw_ref[j:j + 1, :][None] * ext_sc[:, j:j + s_new, :]
    for s in range(n_seq):
        c_sc[s * s_new:(s + 1) * s_new, :] = acc[s]
    st_out_ref[...] = ext_sc[:, s_new:s_new + keep, :]
    c = c_sc[...] + bdw_ref[...]
    y_ref[...] = x + _ln_swish_pointwise(c, lng_ref, lnb_ref, w2_ref, b2_ref)


def _conv_sample(x, state, s_new, g, w1_bf, b1, wdw, bdw, lng, lnb, w2_bf, b2):
    n, d = x.shape
    bd = state.shape[0]
    width = wdw.shape[0]
    keep = width - 1
    ns = _tile(CONV_SEQ_BLOCK, bd)
    y, st = pl.pallas_call(
        functools.partial(_conv_s_body, s_new=s_new, width=width, n_seq=ns),
        grid=(bd // ns,),
        in_specs=[
            pl.BlockSpec((ns * s_new, d), lambda i: (i, 0)),
            pl.BlockSpec((ns, keep, d), lambda i: (i, 0, 0)),
            _const_spec((1, d)),
            _const_spec((d, 2 * d)),
            _const_spec((1, 2 * d)),
            _const_spec((width, d)),
            _const_spec((1, d)),
            _const_spec((1, d)),
            _const_spec((1, d)),
            _const_spec((d, d)),
            _const_spec((1, d)),
        ],
        out_specs=[
            pl.BlockSpec((ns * s_new, d), lambda i: (i, 0)),
            pl.BlockSpec((ns, keep, d), lambda i: (i, 0, 0)),
        ],
        out_shape=[
            jax.ShapeDtypeStruct((n, d), F32),
            jax.ShapeDtypeStruct((bd, keep, d), F32),
        ],
        scratch_shapes=[
            pltpu.VMEM((ns, keep + s_new, d), F32),
            pltpu.VMEM((ns * s_new, d), F32),
        ],
        compiler_params=_cparams("parallel"),
        name="conv_s",
    )(x, state, g, w1_bf, b1, wdw, bdw, lng, lnb, w2_bf, b2)
    return y, st


def _router_body(x_ref, g_ref, wrt_ref, tri_ref, idx_ref, gate_ref, rank_ref, cnt_ref, cnt_sc):
    i = pl.program_id(0)

    @pl.when(i == 0)
    def _():
        cnt_sc[...] = jnp.zeros(cnt_sc.shape, F32)

    h = _rms(x_ref[...], g_ref[...])
    logits = lax.dot_general(wrt_ref[...], h, (((1,), (1,)), ((), ())),
                             precision=lax.Precision.HIGHEST,
                             preferred_element_type=F32)
    n_e = float(logits.shape[0])
    eidx = lax.broadcasted_iota(jnp.int32, logits.shape, 0).astype(F32)
    v1 = jnp.max(logits, axis=0, keepdims=True)
    i1 = jnp.min(jnp.where(logits == v1, eidx, n_e), axis=0, keepdims=True)
    rest = jnp.where(eidx == i1, -jnp.inf, logits)
    v2 = jnp.max(rest, axis=0, keepdims=True)
    i2 = jnp.min(jnp.where(rest == v2, eidx, n_e), axis=0, keepdims=True)
    e = jnp.exp(v2 - v1)
    g1 = 1.0 / (1.0 + e)
    idx_ref[0:1, :] = i1.astype(jnp.int32)
    idx_ref[1:2, :] = i2.astype(jnp.int32)
    gate_ref[0:1, :] = g1
    gate_ref[1:2, :] = e * g1

    hot1 = (eidx == i1).astype(F32)
    hot2 = (eidx == i2).astype(F32)
    tri = tri_ref[...]
    incl1 = jnp.dot(hot1.astype(BF16), tri, preferred_element_type=F32)
    incl2 = jnp.dot(hot2.astype(BF16), tri, preferred_element_type=F32)
    cnt = cnt_sc[...]
    c1 = jnp.sum(hot1, axis=1, keepdims=True)
    c2 = jnp.sum(hot2, axis=1, keepdims=True)
    r1 = jnp.sum(hot1 * (cnt + incl1), axis=0, keepdims=True) - 1.0
    r2 = jnp.sum(hot2 * (cnt + c1 + incl2), axis=0, keepdims=True) - 1.0
    rank_ref[0:1, :] = r1.astype(jnp.int32)
    rank_ref[1:2, :] = r2.astype(jnp.int32)
    cnt_new = cnt + c1 + c2
    cnt_sc[...] = cnt_new
    cnt_ref[...] = cnt_new.astype(jnp.int32)


def _router(x, g, w_router_t):
    n, d = x.shape
    n_e = w_router_t.shape[0]
    tm = _tile(ROUTER_TILE, n)
    tri = (jnp.arange(tm)[:, None] <= jnp.arange(tm)[None, :]).astype(BF16)
    tok = pl.BlockSpec((TOP_K, tm), lambda i: (0, i))
    return pl.pallas_call(
        _router_body,
        grid=(n // tm,),
        in_specs=[
            pl.BlockSpec((tm, d), lambda i: (i, 0)),
            _const_spec((1, d)),
            _const_spec((n_e, d)),
            _const_spec((tm, tm)),
        ],
        out_specs=[tok, tok, tok, pl.BlockSpec((n_e, 1), lambda i: (0, 0))],
        out_shape=[
            jax.ShapeDtypeStruct((TOP_K, n), jnp.int32),
            jax.ShapeDtypeStruct((TOP_K, n), F32),
            jax.ShapeDtypeStruct((TOP_K, n), jnp.int32),
            jax.ShapeDtypeStruct((n_e, 1), jnp.int32),
        ],
        scratch_shapes=[pltpu.VMEM((n_e, 1), F32)],
        compiler_params=_cparams("arbitrary"),
        name="router",
    )(x, g, w_router_t, tri)


def _dispatch_body(dest_ref, x_ref, g_ref, gate_ref, xs_in, xs_out, buf_sc, sem, *, tm, d):
    del xs_in
    i = pl.program_id(0)
    slot = lax.rem(i, 2)
    h = _rms(x_ref[...], g_ref[...])
    gate = gate_ref[...]
    for k in range(TOP_K):
        buf_sc[slot, k, :, 0:d] = h
        buf_sc[slot, k, :, d:] = jnp.broadcast_to(gate[:, k:k + 1], (tm, GATE_LANES))

    def issue(t, carry):
        for k in range(TOP_K):
            dst = dest_ref[0, k * tm + t]
            pltpu.make_async_copy(buf_sc.at[slot, k, pl.ds(t, 1), :],
                                  xs_out.at[pl.ds(dst, 1), :], sem.at[slot]).start()
        return carry

    lax.fori_loop(0, tm, issue, 0, unroll=8)

    def wait_rows(which):
        for k in range(TOP_K):
            pltpu.make_async_copy(buf_sc.at[which, k], xs_out.at[pl.ds(0, tm), :],
                                  sem.at[which]).wait()

    @pl.when(i > 0)
    def _():
        wait_rows(1 - slot)

    @pl.when(i == pl.num_programs(0) - 1)
    def _():
        wait_rows(slot)


def _dispatch(x, g, gate_cols, dest_tiles, p_rows, tm):
    n, d = x.shape
    dg = d + GATE_LANES
    xs0 = jnp.zeros((p_rows, dg), F32)
    return pl.pallas_call(
        functools.partial(_dispatch_body, tm=tm, d=d),
        grid=(n // tm,),
        in_specs=[
            pl.BlockSpec((None, 1, TOP_K * tm), lambda i: (i, 0, 0), memory_space=pltpu.SMEM),
            pl.BlockSpec((tm, d), lambda i: (i, 0)),
            _const_spec((1, d)),
            pl.BlockSpec((tm, TOP_K), lambda i: (i, 0)),
            pl.BlockSpec(memory_space=pl.ANY),
        ],
        out_specs=pl.BlockSpec(memory_space=pl.ANY),
        out_shape=jax.ShapeDtypeStruct((p_rows, dg), F32),
        scratch_shapes=[
            pltpu.VMEM((2, TOP_K, tm, dg), F32),
            pltpu.SemaphoreType.DMA((2,)),
        ],
        input_output_aliases={4: 0},
        compiler_params=_cparams("arbitrary"),
        name="dispatch",
    )(dest_tiles, x, g, gate_cols, xs0)


def _moe_body(te_ref, tv_ref, ts_ref, x_ref, wg_ref, wu_ref, wd_ref, y_ref, xb_sc, acc_sc):
    t = pl.program_id(0)
    c = pl.program_id(1)
    d = acc_sc.shape[1]

    @pl.when(tv_ref[t] == 1)
    def _():
        @pl.when(c == 0)
        def _():
            acc_sc[...] = jnp.zeros(acc_sc.shape, F32)
            xb_sc[...] = x_ref[:, 0:d].astype(BF16)

        x = xb_sc[...]
        gate = jnp.dot(x, wg_ref[...].astype(BF16), preferred_element_type=F32)
        up = jnp.dot(x, wu_ref[...].astype(BF16), preferred_element_type=F32)
        a = (_silu(gate) * up).astype(BF16)
        acc_sc[...] += jnp.dot(a, wd_ref[...].astype(BF16), preferred_element_type=F32)

        @pl.when(c == pl.num_programs(1) - 1)
        def _():
            y_ref[...] = acc_sc[...] * x_ref[:, d:d + 1]

    @pl.when((tv_ref[t] == 0) & (c == pl.num_programs(1) - 1))
    def _():
        y_ref[...] = jnp.zeros(y_ref.shape, F32)


def _moe(xs, tile_expert, tile_valid, tile_src, w_gate_e, w_up_e, w_down_e, tm):
    p_rows, dg = xs.shape
    n_e, d, f = w_gate_e.shape
    fc = _tile(MOE_FF_CHUNK, f)
    nc = f // fc
    n_tiles = p_rows // tm

    def chunk_of(t, c, tv):
        return jnp.where(tv[t] == 1, c, nc - 1)

    grid_spec = pltpu.PrefetchScalarGridSpec(
        num_scalar_prefetch=3,
        grid=(n_tiles, nc),
        in_specs=[
            pl.BlockSpec((tm, dg), lambda t, c, te, tv, ts: (ts[t], 0)),
            pl.BlockSpec((None, d, fc), lambda t, c, te, tv, ts: (te[t], 0, chunk_of(t, c, tv))),
            pl.BlockSpec((None, d, fc), lambda t, c, te, tv, ts: (te[t], 0, chunk_of(t, c, tv))),
            pl.BlockSpec((None, fc, d), lambda t, c, te, tv, ts: (te[t], chunk_of(t, c, tv), 0)),
        ],
        out_specs=pl.BlockSpec((tm, d), lambda t, c, te, tv, ts: (t, 0)),
        scratch_shapes=[pltpu.VMEM((tm, d), BF16), pltpu.VMEM((tm, d), F32)],
    )
    return pl.pallas_call(
        _moe_body,
        grid_spec=grid_spec,
        out_shape=jax.ShapeDtypeStruct((p_rows, d), F32),
        compiler_params=_cparams("arbitrary", "arbitrary"),
        name="moe",
    )(tile_expert, tile_valid, tile_src, xs, w_gate_e, w_up_e, w_down_e)


def _moe_layer(x, g, w_router, w_gate_e, w_up_e, w_down_e):
    n, d = x.shape
    n_e = w_router.shape[1]
    tm = min(MOE_TILE, n)
    idx, gates, rank, counts = _router(x, g, w_router.T)

    counts = counts[:, 0]
    padded = ((counts + tm - 1) // tm) * tm
    ends = jnp.cumsum(padded)
    offs = ends - padded
    off_of = jnp.zeros_like(idx)
    for e in range(n_e):
        off_of = jnp.where(idx == e, offs[e], off_of)
    dest = off_of + rank
    n_tiles = (TOP_K * n) // tm + n_e
    p_rows = n_tiles * tm
    tile_start = jnp.arange(n_tiles, dtype=jnp.int32) * tm
    tile_valid = (tile_start < ends[-1]).astype(jnp.int32)
    last_valid = jnp.maximum(jnp.sum(tile_valid) - 1, 0)
    tile_src = jnp.minimum(jnp.arange(n_tiles, dtype=jnp.int32), last_valid)
    tile_expert = jnp.minimum(
        jnp.sum((tile_src * tm)[:, None] >= ends[None, :], axis=1), n_e - 1).astype(jnp.int32)

    td = _tile(DISPATCH_TILE, n)
    dest_tiles = dest.reshape(TOP_K, n // td, td).transpose(1, 0, 2).reshape(n // td, 1, TOP_K * td)
    xs = _dispatch(x, g, gates.T, dest_tiles, p_rows, td)
    ys = _moe(xs, tile_expert, tile_valid, tile_src, w_gate_e, w_up_e, w_down_e, tm)
    return x + (jnp.take(ys, dest[0], axis=0) + jnp.take(ys, dest[1], axis=0))


def kernel(x_prompt, x_sample, cache_k, cache_v, state_conv, page_table, norm_attn, w_qkv, q_norm, k_norm, lambda_q1, lambda_k1, lambda_q2, lambda_k2, subln_g, w_o, norm_ffn0, w_gate0, w_up0, w_down0, norm_conv, w_pw1, b_pw1, w_dw, b_dw, ln_g, ln_b, w_pw2, b_pw2, norm_ffn1, w_router, w_gate_e, w_up_e, w_down_e):
    batch, seq, d = x_prompt.shape
    bd, s_new, _ = x_sample.shape
    n_heads = cache_k.shape[2]
    head_dim = q_norm.shape[0]
    hw = 2 * head_dim
    n_p = batch * seq
    n_s = bd * s_new

    row = lambda v: v.reshape(1, -1)
    bf = lambda w: w.astype(BF16)
    xp = x_prompt.reshape(n_p, d)
    xs = x_sample.reshape(n_s, d)
    grp = jnp.kron(jnp.eye(d // head_dim, dtype=F32), jnp.ones((head_dim, head_dim), F32)).astype(BF16)
    qn_t = row(jnp.tile(q_norm, d // head_dim))
    kn_t = row(jnp.tile(k_norm, d // head_dim))
    lam_vecs = (row(lambda_q1), row(lambda_k1), row(lambda_q2), row(lambda_k2))
    sg = row(subln_g)
    w_qkv_bf = bf(w_qkv)

    qk_scale = head_dim ** -0.5
    qp, kp, vp, kpb, vpb = _qkv(xp, row(norm_attn), w_qkv_bf, qn_t, kn_t, grp, head_dim, BF16,
                                qk_scale * math.log2(math.e))
    qs, ks, vs, _, _ = _qkv(xs, row(norm_attn), w_qkv_bf, qn_t, kn_t, grp, head_dim, F32, qk_scale)
    op = _attn_prompt(qp, kpb, vpb, lam_vecs, sg, batch, seq, n_heads, head_dim)
    osm = _attn_sample(qs.reshape(bd, s_new, d), ks.reshape(bd, s_new, d), vs.reshape(bd, s_new, d),
                       cache_k, cache_v, page_table, lam_vecs, sg, n_heads, head_dim)

    ffn0_w = (bf(w_o), row(norm_ffn0), bf(w_gate0), bf(w_up0), bf(w_down0))
    xp = _ffn0(xp, op, *ffn0_w)
    xs = _ffn0(xs, osm.reshape(n_s, d), *ffn0_w)

    conv_w = (row(norm_conv), bf(w_pw1), row(b_pw1), w_dw, row(b_dw), row(ln_g), row(ln_b),
              bf(w_pw2), row(b_pw2))
    xp, conv_prompt = _conv_prompt(xp, batch, seq, *conv_w)
    xs, conv_sample = _conv_sample(xs, state_conv, s_new, *conv_w)

    x_all = _moe_layer(jnp.concatenate([xp, xs], axis=0), row(norm_ffn1), w_router,
                       w_gate_e, w_up_e, w_down_e)

    return (x_all[:n_p].reshape(batch, seq, d), x_all[n_p:].reshape(bd, s_new, d),
            kp.reshape(batch, seq, n_heads, hw), vp.reshape(batch, seq, n_heads, hw),
            ks.reshape(bd, s_new, n_heads, hw), vs.reshape(bd, s_new, n_heads, hw),
            conv_prompt, conv_sample)
```

```python
import functools
import math

import jax
import jax.numpy as jnp
from jax import lax
from jax.experimental import pallas as pl
from jax.experimental.pallas import tpu as pltpu

F32 = jnp.float32
BF16 = jnp.bfloat16

EPS = 1e-6
LAMBDA_INIT = 0.8 - 0.6 * math.exp(-0.3 * 0)
TOP_K = 2
NEG = -1e30
SUBLANES = 8
LANES = 128

VMEM_LIMIT_BYTES = 56 * 1024 * 1024

ROW_TILE_QKV = 256
ROW_TILE_FFN = 512
ATTN_TILE = 512
ATTN_HEADS_PER_STEP = 8
ATTN_ROW_CHUNK = 512
PAGES_PER_STEP = 16
CONV_TILE = 512
CONV_HALO = 32
CONV_ROWS = 16
CONV_SEQ_BLOCK = 32
ROUTER_TILE = 512
MOE_TILE = 1024
MOE_FF_CHUNK = 512
DISPATCH_TILE = 256
GATE_LANES = 128


def _cparams(*sem):
    return pltpu.CompilerParams(dimension_semantics=sem, vmem_limit_bytes=VMEM_LIMIT_BYTES)


def _const_spec(shape):
    nd = len(shape)
    return pl.BlockSpec(shape, lambda *_: (0,) * nd, pipeline_mode=pl.Buffered(1))


def _rms(x, g):
    return x * lax.rsqrt(jnp.mean(x * x, axis=-1, keepdims=True) + EPS) * g


def _silu(x):
    return x * jax.nn.sigmoid(x)


def _tile(limit, n):
    t = min(limit, n)
    assert n % t == 0, (n, t)
    return t


def _div_pow2(x, c):
    assert c > 0 and c & (c - 1) == 0, c
    return lax.shift_right_logical(x, c.bit_length() - 1)


def _mod_pow2(x, c):
    assert c > 0 and c & (c - 1) == 0, c
    return lax.bitwise_and(x, c - 1)


def _qkv_body(x_ref, g_ref, w_ref, qn_ref, kn_ref, grp_ref, q_out, k_out, v_out, kb_out, vb_out,
              *, d_model, head_dim, q_scale):
    x = x_ref[...]
    h = _rms(x, g_ref[...]).astype(BF16)
    qkv = jnp.dot(h, w_ref[...], preferred_element_type=F32)
    q = qkv[:, :d_model]
    k = qkv[:, d_model:2 * d_model]
    v = qkv[:, 2 * d_model:]
    grp = grp_ref[...]

    def head_norm(t, gain):
        ssq = jnp.dot((t * t).astype(BF16), grp, preferred_element_type=F32)
        return t * lax.rsqrt(ssq * (1.0 / head_dim) + EPS) * gain

    qn = head_norm(q, qn_ref[...]) * q_scale
    kn = head_norm(k, kn_ref[...])
    q_out[...] = qn.astype(q_out.dtype)
    k_out[...] = kn
    v_out[...] = v
    kb_out[...] = kn.astype(BF16)
    vb_out[...] = v.astype(BF16)


def _qkv(x, g, w_bf, qn_t, kn_t, grp, head_dim, q_dtype, q_scale):
    n, d = x.shape
    tm = _tile(ROW_TILE_QKV, n)
    row = lambda i: (i, 0)
    out = pl.pallas_call(
        functools.partial(_qkv_body, d_model=d, head_dim=head_dim, q_scale=q_scale),
        grid=(n // tm,),
        in_specs=[
            pl.BlockSpec((tm, d), row),
            _const_spec((1, d)),
            _const_spec((d, 3 * d)),
            _const_spec((1, d)),
            _const_spec((1, d)),
            _const_spec((d, d)),
        ],
        out_specs=[pl.BlockSpec((tm, d), row)] * 5,
        out_shape=[
            jax.ShapeDtypeStruct((n, d), q_dtype),
            jax.ShapeDtypeStruct((n, d), F32),
            jax.ShapeDtypeStruct((n, d), F32),
            jax.ShapeDtypeStruct((n, d), BF16),
            jax.ShapeDtypeStruct((n, d), BF16),
        ],
        compiler_params=_cparams("parallel"),
        name="qkv",
    )(x, g, w_bf, qn_t, kn_t, grp)
    return out


def _lambda(lq1, lk1, lq2, lk2):
    a = jnp.sum(lq1[...] * lk1[...], axis=-1, keepdims=True)
    b = jnp.sum(lq2[...] * lk2[...], axis=-1, keepdims=True)
    return jnp.exp(a) - jnp.exp(b) + LAMBDA_INIT


def _attn_p_body(qi_tab, ki_tab, q_ref, k_ref, v_ref, lq1, lk1, lq2, lk2, sg_ref, o_ref,
                 qq_sc, m_sc, l_sc, acc_sc, *, tq, head_dim, heads):
    step = pl.program_id(2)
    qi = qi_tab[step]
    ki = ki_tab[step]
    hw = 2 * head_dim
    rc = _tile(ATTN_ROW_CHUNK, tq)

    @pl.when(ki == 0)
    def _():
        for h in range(heads):
            q = q_ref[:, h * hw:(h + 1) * hw]
            lane = lax.broadcasted_iota(jnp.int32, q.shape, 1)
            zero = jnp.zeros_like(q)
            qq_sc[h, 0:tq, :] = jnp.where(lane < head_dim, q, zero)
            qq_sc[h, tq:2 * tq, :] = jnp.where(lane >= head_dim, q, zero)
        m_sc[...] = jnp.full(m_sc.shape, NEG, F32)
        l_sc[...] = jnp.zeros(l_sc.shape, F32)
        acc_sc[...] = jnp.zeros(acc_sc.shape, F32)

    def update(diagonal):
        for h in range(heads):
            lanes = slice(h * hw, (h + 1) * hw)
            for r0 in range(0, 2 * tq, rc):
                q0 = r0 % tq
                nk = q0 + rc if diagonal else tq
                rows = slice(r0, r0 + rc)
                s = lax.dot_general(qq_sc[h, rows, :], k_ref[0:nk, lanes], (((1,), (1,)), ((), ())),
                                    preferred_element_type=F32)
                if diagonal:
                    row = lax.broadcasted_iota(jnp.int32, s.shape, 0) + q0
                    col = lax.broadcasted_iota(jnp.int32, s.shape, 1)
                    s = jnp.where(col <= row, s, NEG)
                m_prev = m_sc[h, rows, :]
                m_new = jnp.maximum(m_prev, jnp.max(s, axis=-1, keepdims=True))
                alpha = jnp.exp2(m_prev - m_new)
                p = jnp.exp2(s - jnp.concatenate([m_new] * (nk // hw), axis=1)).astype(BF16)
                v1 = jnp.concatenate([v_ref[0:nk, lanes], jnp.ones((nk, hw), BF16)], axis=1)
                pv = jnp.dot(p, v1, preferred_element_type=F32)
                l_sc[h, rows, :] = alpha * l_sc[h, rows, :] + pv[:, hw:]
                acc_sc[h, rows, :] = alpha * acc_sc[h, rows, :] + pv[:, :hw]
                m_sc[h, rows, :] = m_new

    @pl.when(ki < qi)
    def _():
        update(False)

    @pl.when(ki == qi)
    def _():
        update(True)
        lam = _lambda(lq1, lk1, lq2, lk2)
        for h in range(heads):
            o1 = acc_sc[h, 0:tq, :] / l_sc[h, 0:tq, :]
            o2 = acc_sc[h, tq:2 * tq, :] / l_sc[h, tq:2 * tq, :]
            o = o1 - lam * o2
            o_ref[:, h * hw:(h + 1) * hw] = (_rms(o, sg_ref[...]) * (1.0 - LAMBDA_INIT)).astype(o_ref.dtype)


def _attn_prompt(q_bf, k_bf, v_bf, lam_vecs, subln_g, batch, seq, n_heads, head_dim):
    n, d = q_bf.shape
    hw = 2 * head_dim
    tq = _tile(ATTN_TILE, seq)
    nq = seq // tq
    heads = _tile(ATTN_HEADS_PER_STEP, n_heads)
    pairs = [(i, j) for i in range(nq) for j in range(i + 1)]
    qi_tab = jnp.asarray([p[0] for p in pairs], jnp.int32)
    ki_tab = jnp.asarray([p[1] for p in pairs], jnp.int32)
    q_map = lambda b, h, s, qt, kt: (b * nq + qt[s], h)
    kv_map = lambda b, h, s, qt, kt: (b * nq + kt[s], h)
    vec = pl.BlockSpec((1, head_dim), lambda b, h, s, qt, kt: (0, 0))
    grid_spec = pltpu.PrefetchScalarGridSpec(
        num_scalar_prefetch=2,
        grid=(batch, n_heads // heads, len(pairs)),
        in_specs=[
            pl.BlockSpec((tq, heads * hw), q_map),
            pl.BlockSpec((tq, heads * hw), kv_map),
            pl.BlockSpec((tq, heads * hw), kv_map),
            vec, vec, vec, vec,
            pl.BlockSpec((1, hw), lambda b, h, s, qt, kt: (0, 0)),
        ],
        out_specs=pl.BlockSpec((tq, heads * hw), q_map),
        scratch_shapes=[
            pltpu.VMEM((heads, 2 * tq, hw), BF16),
            pltpu.VMEM((heads, 2 * tq, hw), F32),
            pltpu.VMEM((heads, 2 * tq, hw), F32),
            pltpu.VMEM((heads, 2 * tq, hw), F32),
        ],
    )
    return pl.pallas_call(
        functools.partial(_attn_p_body, tq=tq, head_dim=head_dim, heads=heads),
        grid_spec=grid_spec,
        out_shape=jax.ShapeDtypeStruct((n, d), BF16),
        compiler_params=_cparams("parallel", "parallel", "arbitrary"),
        name="attn_p",
    )(qi_tab, ki_tab, q_bf, k_bf, v_bf, *lam_vecs, subln_g)


def _attn_s_body(pt_ref, q_ref, kn_ref, vn_ref, lq1, lk1, lq2, lk2, sg_ref, *rest,
                 n_pages_step, n_heads, head_dim, s_new):
    k_refs = rest[:n_pages_step]
    v_refs = rest[n_pages_step:2 * n_pages_step]
    o_ref = rest[2 * n_pages_step]
    qm_sc, bias_sc, m_sc, l_sc, acc_sc = rest[2 * n_pages_step + 1:]
    j = pl.program_id(1)
    rows, hw = acc_sc.shape
    per_head = 2 * s_new
    nt = (((1,), (1,)), ((), ()))

    @pl.when(j == 0)
    def _():
        q = q_ref[...]
        r8 = lax.broadcasted_iota(jnp.int32, (per_head, hw), 0)
        l8 = lax.broadcasted_iota(jnp.int32, (per_head, hw), 1)
        own_map = _div_pow2(l8, head_dim) == _div_pow2(r8, s_new)
        blocks = []
        for h in range(n_heads):
            qh = q[:, h * hw:(h + 1) * hw]
            blk = jnp.zeros((per_head, hw), F32)
            for t in range(s_new):
                blk = jnp.where(own_map & (_mod_pow2(r8, s_new) == t), qh[t:t + 1, :], blk)
            blocks.append(blk)
        qm = jnp.concatenate(blocks, axis=0).astype(BF16)
        qm_sc[...] = qm
        ridx = lax.broadcasted_iota(jnp.int32, bias_sc.shape, 0)
        col = lax.broadcasted_iota(jnp.int32, bias_sc.shape, 1)
        bias_sc[...] = jnp.where(_mod_pow2(col, n_heads) == _div_pow2(ridx, per_head), 0.0, NEG)
        n_new = kn_ref.shape[0]
        s = lax.dot_general(qm, kn_ref[...].astype(BF16), nt, preferred_element_type=F32)
        ridx = lax.broadcasted_iota(jnp.int32, (rows, n_new), 0)
        col = lax.broadcasted_iota(jnp.int32, (rows, n_new), 1)
        ok = (_mod_pow2(col, n_heads) == _div_pow2(ridx, per_head)) & (
            _div_pow2(col, n_heads) <= _mod_pow2(ridx, s_new))
        s = jnp.where(ok, s, NEG)
        m = jnp.max(s, axis=-1, keepdims=True)
        p = jnp.exp(s - m)
        m_sc[...] = m
        l_sc[...] = jnp.sum(p, axis=-1, keepdims=True)
        acc_sc[...] = jnp.dot(p.astype(BF16), vn_ref[...].astype(BF16), preferred_element_type=F32)

    qm = qm_sc[...]
    bias = bias_sc[...]
    scores = [lax.dot_general(qm, k_refs[i][...].astype(BF16), nt, preferred_element_type=F32) + bias
              for i in range(n_pages_step)]
    m_prev = m_sc[...]
    m_new = m_prev
    for s in scores:
        m_new = jnp.maximum(m_new, jnp.max(s, axis=-1, keepdims=True))
    alpha = jnp.exp(m_prev - m_new)
    l = alpha * l_sc[...]
    acc = alpha * acc_sc[...]
    for i, s in enumerate(scores):
        p = jnp.exp(s - m_new)
        l = l + jnp.sum(p, axis=-1, keepdims=True)
        acc = acc + jnp.dot(p.astype(BF16), v_refs[i][...].astype(BF16), preferred_element_type=F32)
    m_sc[...] = m_new
    l_sc[...] = l
    acc_sc[...] = acc

    @pl.when(j == pl.num_programs(1) - 1)
    def _():
        accn = acc / l
        lam = _lambda(lq1, lk1, lq2, lk2)
        sg = sg_ref[...]
        for h in range(n_heads):
            blk = accn[h * per_head:(h + 1) * per_head]
            oh = blk[0:s_new] - lam * blk[s_new:per_head]
            o_ref[:, h * hw:(h + 1) * hw] = _rms(oh, sg) * (1.0 - LAMBDA_INIT)


def _attn_sample(q, k_new, v_new, cache_k, cache_v, page_table, lam_vecs, subln_g, n_heads, head_dim):
    bd, s_new, d = q.shape
    n_phys, page, _, hw = cache_k.shape
    prow = page * n_heads
    ck = cache_k.reshape(n_phys, prow, hw)
    cv = cache_v.reshape(n_phys, prow, hw)
    n_pages = page_table.shape[1]
    pps = _tile(PAGES_PER_STEP, n_pages)
    rows = n_heads * 2 * s_new
    seq_spec = pl.BlockSpec((None, s_new, d), lambda b, j, pt: (b, 0, 0))
    new_spec = pl.BlockSpec((None, s_new * n_heads, hw), lambda b, j, pt: (b, 0, 0))
    vec = pl.BlockSpec((1, head_dim), lambda b, j, pt: (0, 0))

    def page_spec(i):
        return pl.BlockSpec((None, prow, hw), lambda b, j, pt: (pt[b, j * pps + i], 0, 0))

    grid_spec = pltpu.PrefetchScalarGridSpec(
        num_scalar_prefetch=1,
        grid=(bd, n_pages // pps),
        in_specs=[seq_spec, new_spec, new_spec, vec, vec, vec, vec,
                  pl.BlockSpec((1, hw), lambda b, j, pt: (0, 0))]
                 + [page_spec(i) for i in range(pps)] + [page_spec(i) for i in range(pps)],
        out_specs=seq_spec,
        scratch_shapes=[
            pltpu.VMEM((rows, hw), BF16),
            pltpu.VMEM((rows, prow), F32),
            pltpu.VMEM((rows, 1), F32),
            pltpu.VMEM((rows, 1), F32),
            pltpu.VMEM((rows, hw), F32),
        ],
    )
    return pl.pallas_call(
        functools.partial(_attn_s_body, n_pages_step=pps, n_heads=n_heads, head_dim=head_dim,
                          s_new=s_new),
        grid_spec=grid_spec,
        out_shape=jax.ShapeDtypeStruct((bd, s_new, d), F32),
        compiler_params=_cparams("parallel", "arbitrary"),
        name="attn_s",
    )(page_table, q, k_new.reshape(bd, s_new * n_heads, hw), v_new.reshape(bd, s_new * n_heads, hw),
      *lam_vecs, subln_g, *([ck] * pps), *([cv] * pps))


def _ffn0_body(x_ref, o_ref, wo_ref, g_ref, wg_ref, wu_ref, wd_ref, y_ref, *, ff_chunks):
    x1 = x_ref[...] + jnp.dot(o_ref[...].astype(BF16), wo_ref[...], preferred_element_type=F32)
    h = _rms(x1, g_ref[...]).astype(BF16)
    y = x1
    for lo, hi in ff_chunks:
        gate = jnp.dot(h, wg_ref[:, lo:hi], preferred_element_type=F32)
        up = jnp.dot(h, wu_ref[:, lo:hi], preferred_element_type=F32)
        a = (_silu(gate) * up).astype(BF16)
        y = y + jnp.dot(a, wd_ref[lo:hi, :], preferred_element_type=F32)
    y_ref[...] = y


def _ffn0(x, o, wo_bf, g, wg_bf, wu_bf, wd_bf):
    n, d = x.shape
    f = wg_bf.shape[1]
    tm = _tile(ROW_TILE_FFN, n)
    half = (f // 256) * 128
    ff_chunks = ((0, half), (half, f))
    row = lambda i: (i, 0)
    return pl.pallas_call(
        functools.partial(_ffn0_body, ff_chunks=ff_chunks),
        grid=(n // tm,),
        in_specs=[
            pl.BlockSpec((tm, d), row),
            pl.BlockSpec((tm, d), row),
            _const_spec((d, d)),
            _const_spec((1, d)),
            _const_spec((d, f)),
            _const_spec((d, f)),
            _const_spec((f, d)),
        ],
        out_specs=pl.BlockSpec((tm, d), row),
        out_shape=jax.ShapeDtypeStruct((n, d), F32),
        compiler_params=_cparams("parallel"),
        name="ffn0",
    )(x, o, wo_bf, g, wg_bf, wu_bf, wd_bf)


def _glu_pointwise(x, g_ref, w1_ref, b1_ref, d):
    h = _rms(x, g_ref[...]).astype(BF16)
    u = jnp.dot(h, w1_ref[...], preferred_element_type=F32) + b1_ref[...]
    return u[:, :d] * jax.nn.sigmoid(u[:, d:])


def _ln_swish_pointwise(c, lng_ref, lnb_ref, w2_ref, b2_ref):
    mu = jnp.mean(c, axis=-1, keepdims=True)
    xc = c - mu
    yln = xc * lax.rsqrt(jnp.mean(xc * xc, axis=-1, keepdims=True) + EPS) * lng_ref[...] + lnb_ref[...]
    return jnp.dot(_silu(yln).astype(BF16), w2_ref[...], preferred_element_type=F32) + b2_ref[...]


def _conv_p_body(x_ref, g_ref, w1_ref, b1_ref, wdw_ref, bdw_ref, lng_ref, lnb_ref, w2_ref, b2_ref,
                 y_ref, st_ref, u_sc, sh_sc, c_sc, *, tt, width):
    t = pl.program_id(1)
    d = x_ref.shape[-1]
    keep = width - 1

    @pl.when(t == 0)
    def _():
        u_sc[0:CONV_HALO, :] = jnp.zeros((CONV_HALO, d), F32)

    x = x_ref[...]
    u_sc[CONV_HALO:CONV_HALO + tt, :] = _glu_pointwise(x, g_ref, w1_ref, b1_ref, d)

    base = CONV_HALO - keep
    span = sh_sc.shape[1]
    for b in range(1, SUBLANES):
        sh_sc[b - 1, :, :] = u_sc[b:b + span, :]

    def chunk(r, carry):
        r0 = pl.multiple_of(r * CONV_ROWS, CONV_ROWS)
        for c0 in range(0, d, LANES):
            lanes = slice(c0, c0 + LANES)
            acc = jnp.zeros((CONV_ROWS, LANES), F32)
            for j in range(width):
                a, b = divmod(base + j, SUBLANES)
                start = pl.multiple_of(r0 + SUBLANES * a, SUBLANES)
                if b == 0:
                    tap = u_sc[pl.ds(start, CONV_ROWS), lanes]
                else:
                    tap = sh_sc[b - 1, pl.ds(start, CONV_ROWS), lanes]
                acc = acc + wdw_ref[j:j + 1, lanes] * tap
            c_sc[pl.ds(r0, CONV_ROWS), lanes] = acc
        return carry

    lax.fori_loop(0, tt // CONV_ROWS, chunk, 0)

    c = c_sc[...] + bdw_ref[...]
    y_ref[...] = x + _ln_swish_pointwise(c, lng_ref, lnb_ref, w2_ref, b2_ref)
    st_ref[...] = u_sc[CONV_HALO + tt - keep:CONV_HALO + tt, :]
    u_sc[0:CONV_HALO, :] = u_sc[tt:tt + CONV_HALO, :]


def _conv_prompt(x, batch, seq, g, w1_bf, b1, wdw, bdw, lng, lnb, w2_bf, b2):
    n, d = x.shape
    width = wdw.shape[0]
    tt = _tile(CONV_TILE, seq)
    nt = seq // tt
    row = lambda b, t: (b * nt + t, 0)
    y, st = pl.pallas_call(
        functools.partial(_conv_p_body, tt=tt, width=width),
        grid=(batch, nt),
        in_specs=[
            pl.BlockSpec((tt, d), row),
            _const_spec((1, d)),
            _const_spec((d, 2 * d)),
            _const_spec((1, 2 * d)),
            _const_spec((width, d)),
            _const_spec((1, d)),
            _const_spec((1, d)),
            _const_spec((1, d)),
            _const_spec((d, d)),
            _const_spec((1, d)),
        ],
        out_specs=[
            pl.BlockSpec((tt, d), row),
            pl.BlockSpec((None, width - 1, d), lambda b, t: (b, 0, 0)),
        ],
        out_shape=[
            jax.ShapeDtypeStruct((n, d), F32),
            jax.ShapeDtypeStruct((batch, width - 1, d), F32),
        ],
        scratch_shapes=[
            pltpu.VMEM((CONV_HALO + tt, d), F32),
            pltpu.VMEM((SUBLANES - 1, CONV_HALO + tt - SUBLANES, d), F32),
            pltpu.VMEM((tt, d), F32),
        ],
        compiler_params=_cparams("parallel", "arbitrary"),
        name="conv_p",
    )(x, g, w1_bf, b1, wdw, bdw, lng, lnb, w2_bf, b2)
    return y, st


def _conv_s_body(x_ref, st_in_ref, g_ref, w1_ref, b1_ref, wdw_ref, bdw_ref, lng_ref, lnb_ref,
                 w2_ref, b2_ref, y_ref, st_out_ref, ext_sc, c_sc, *, s_new, width, n_seq):
    d = x_ref.shape[-1]
    keep = width - 1
    x = x_ref[...]
    ug = _glu_pointwise(x, g_ref, w1_ref, b1_ref, d)
    ext_sc[:, 0:keep, :] = st_in_ref[...]
    for s in range(n_seq):
        ext_sc[s, keep:keep + s_new, :] = ug[s * s_new:(s + 1) * s_new, :]
    acc = jnp.zeros((n_seq, s_new, d), F32)
    for j in range(width):
        acc = acc + wdw_ref[j:j + 1, :][None] * ext_sc[:, j:j + s_new, :]
    for s in range(n_seq):
        c_sc[s * s_new:(s + 1) * s_new, :] = acc[s]
    st_out_ref[...] = ext_sc[:, s_new:s_new + keep, :]
    c = c_sc[...] + bdw_ref[...]
    y_ref[...] = x + _ln_swish_pointwise(c, lng_ref, lnb_ref, w2_ref, b2_ref)


def _conv_sample(x, state, s_new, g, w1_bf, b1, wdw, bdw, lng, lnb, w2_bf, b2):
    n, d = x.shape
    bd = state.shape[0]
    width = wdw.shape[0]
    keep = width - 1
    ns = _tile(CONV_SEQ_BLOCK, bd)
    y, st = pl.pallas_call(
        functools.partial(_conv_s_body, s_new=s_new, width=width, n_seq=ns),
        grid=(bd // ns,),
        in_specs=[
            pl.BlockSpec((ns * s_new, d), lambda i: (i, 0)),
            pl.BlockSpec((ns, keep, d), lambda i: (i, 0, 0)),
            _const_spec((1, d)),
            _const_spec((d, 2 * d)),
            _const_spec((1, 2 * d)),
            _const_spec((width, d)),
            _const_spec((1, d)),
            _const_spec((1, d)),
            _const_spec((1, d)),
            _const_spec((d, d)),
            _const_spec((1, d)),
        ],
        out_specs=[
            pl.BlockSpec((ns * s_new, d), lambda i: (i, 0)),
            pl.BlockSpec((ns, keep, d), lambda i: (i, 0, 0)),
        ],
        out_shape=[
            jax.ShapeDtypeStruct((n, d), F32),
            jax.ShapeDtypeStruct((bd, keep, d), F32),
        ],
        scratch_shapes=[
            pltpu.VMEM((ns, keep + s_new, d), F32),
            pltpu.VMEM((ns * s_new, d), F32),
        ],
        compiler_params=_cparams("parallel"),
        name="conv_s",
    )(x, state, g, w1_bf, b1, wdw, bdw, lng, lnb, w2_bf, b2)
    return y, st


def _router_body(x_ref, g_ref, wrt_ref, tri_ref, idx_ref, gate_ref, rank_ref, cnt_ref, cnt_sc):
    i = pl.program_id(0)

    @pl.when(i == 0)
    def _():
        cnt_sc[...] = jnp.zeros(cnt_sc.shape, F32)

    h = _rms(x_ref[...], g_ref[...])
    logits = lax.dot_general(wrt_ref[...], h, (((1,), (1,)), ((), ())),
                             precision=lax.Precision.HIGHEST,
                             preferred_element_type=F32)
    n_e = float(logits.shape[0])
    eidx = lax.broadcasted_iota(jnp.int32, logits.shape, 0).astype(F32)
    v1 = jnp.max(logits, axis=0, keepdims=True)
    i1 = jnp.min(jnp.where(logits == v1, eidx, n_e), axis=0, keepdims=True)
    rest = jnp.where(eidx == i1, -jnp.inf, logits)
    v2 = jnp.max(rest, axis=0, keepdims=True)
    i2 = jnp.min(jnp.where(rest == v2, eidx, n_e), axis=0, keepdims=True)
    e = jnp.exp(v2 - v1)
    g1 = 1.0 / (1.0 + e)
    idx_ref[0:1, :] = i1.astype(jnp.int32)
    idx_ref[1:2, :] = i2.astype(jnp.int32)
    gate_ref[0:1, :] = g1
    gate_ref[1:2, :] = e * g1

    hot1 = (eidx == i1).astype(F32)
    hot2 = (eidx == i2).astype(F32)
    tri = tri_ref[...]
    incl1 = jnp.dot(hot1.astype(BF16), tri, preferred_element_type=F32)
    incl2 = jnp.dot(hot2.astype(BF16), tri, preferred_element_type=F32)
    cnt = cnt_sc[...]
    c1 = jnp.sum(hot1, axis=1, keepdims=True)
    c2 = jnp.sum(hot2, axis=1, keepdims=True)
    r1 = jnp.sum(hot1 * (cnt + incl1), axis=0, keepdims=True) - 1.0
    r2 = jnp.sum(hot2 * (cnt + c1 + incl2), axis=0, keepdims=True) - 1.0
    rank_ref[0:1, :] = r1.astype(jnp.int32)
    rank_ref[1:2, :] = r2.astype(jnp.int32)
    cnt_new = cnt + c1 + c2
    cnt_sc[...] = cnt_new
    cnt_ref[...] = cnt_new.astype(jnp.int32)


def _router(x, g, w_router_t):
    n, d = x.shape
    n_e = w_router_t.shape[0]
    tm = _tile(ROUTER_TILE, n)
    tri = (jnp.arange(tm)[:, None] <= jnp.arange(tm)[None, :]).astype(BF16)
    tok = pl.BlockSpec((TOP_K, tm), lambda i: (0, i))
    return pl.pallas_call(
        _router_body,
        grid=(n // tm,),
        in_specs=[
            pl.BlockSpec((tm, d), lambda i: (i, 0)),
            _const_spec((1, d)),
            _const_spec((n_e, d)),
            _const_spec((tm, tm)),
        ],
        out_specs=[tok, tok, tok, pl.BlockSpec((n_e, 1), lambda i: (0, 0))],
        out_shape=[
            jax.ShapeDtypeStruct((TOP_K, n), jnp.int32),
            jax.ShapeDtypeStruct((TOP_K, n), F32),
            jax.ShapeDtypeStruct((TOP_K, n), jnp.int32),
            jax.ShapeDtypeStruct((n_e, 1), jnp.int32),
        ],
        scratch_shapes=[pltpu.VMEM((n_e, 1), F32)],
        compiler_params=_cparams("arbitrary"),
        name="router",
    )(x, g, w_router_t, tri)


def _dispatch_body(dest_ref, x_ref, g_ref, gate_ref, xs_in, xs_out, buf_sc, sem, *, tm, d):
    del xs_in
    i = pl.program_id(0)
    slot = lax.rem(i, 2)
    h = _rms(x_ref[...], g_ref[...])
    gate = gate_ref[...]
    for k in range(TOP_K):
        buf_sc[slot, k, :, 0:d] = h
        buf_sc[slot, k, :, d:] = jnp.broadcast_to(gate[:, k:k + 1], (tm, GATE_LANES))

    def issue(t, carry):
        for k in range(TOP_K):
            dst = dest_ref[0, k * tm + t]
            pltpu.make_async_copy(buf_sc.at[slot, k, pl.ds(t, 1), :],
                                  xs_out.at[pl.ds(dst, 1), :], sem.at[slot]).start()
        return carry

    lax.fori_loop(0, tm, issue, 0, unroll=8)

    def wait_rows(which):
        for k in range(TOP_K):
            pltpu.make_async_copy(buf_sc.at[which, k], xs_out.at[pl.ds(0, tm), :],
                                  sem.at[which]).wait()

    @pl.when(i > 0)
    def _():
        wait_rows(1 - slot)

    @pl.when(i == pl.num_programs(0) - 1)
    def _():
        wait_rows(slot)


def _dispatch(x, g, gate_cols, dest_tiles, p_rows, tm):
    n, d = x.shape
    dg = d + GATE_LANES
    xs0 = jnp.zeros((p_rows, dg), F32)
    return pl.pallas_call(
        functools.partial(_dispatch_body, tm=tm, d=d),
        grid=(n // tm,),
        in_specs=[
            pl.BlockSpec((None, 1, TOP_K * tm), lambda i: (i, 0, 0), memory_space=pltpu.SMEM),
            pl.BlockSpec((tm, d), lambda i: (i, 0)),
            _const_spec((1, d)),
            pl.BlockSpec((tm, TOP_K), lambda i: (i, 0)),
            pl.BlockSpec(memory_space=pl.ANY),
        ],
        out_specs=pl.BlockSpec(memory_space=pl.ANY),
        out_shape=jax.ShapeDtypeStruct((p_rows, dg), F32),
        scratch_shapes=[
            pltpu.VMEM((2, TOP_K, tm, dg), F32),
            pltpu.SemaphoreType.DMA((2,)),
        ],
        input_output_aliases={4: 0},
        compiler_params=_cparams("arbitrary"),
        name="dispatch",
    )(dest_tiles, x, g, gate_cols, xs0)


def _moe_body(te_ref, tv_ref, ts_ref, x_ref, wg_ref, wu_ref, wd_ref, y_ref, xb_sc, acc_sc):
    t = pl.program_id(0)
    c = pl.program_id(1)
    d = acc_sc.shape[1]

    @pl.when(tv_ref[t] == 1)
    def _():
        @pl.when(c == 0)
        def _():
            acc_sc[...] = jnp.zeros(acc_sc.shape, F32)
            xb_sc[...] = x_ref[:, 0:d].astype(BF16)

        x = xb_sc[...]
        gate = jnp.dot(x, wg_ref[...].astype(BF16), preferred_element_type=F32)
        up = jnp.dot(x, wu_ref[...].astype(BF16), preferred_element_type=F32)
        a = (_silu(gate) * up).astype(BF16)
        acc_sc[...] += jnp.dot(a, wd_ref[...].astype(BF16), preferred_element_type=F32)

        @pl.when(c == pl.num_programs(1) - 1)
        def _():
            y_ref[...] = acc_sc[...] * x_ref[:, d:d + 1]

    @pl.when((tv_ref[t] == 0) & (c == pl.num_programs(1) - 1))
    def _():
        y_ref[...] = jnp.zeros(y_ref.shape, F32)


def _moe(xs, tile_expert, tile_valid, tile_src, w_gate_e, w_up_e, w_down_e, tm):
    p_rows, dg = xs.shape
    n_e, d, f = w_gate_e.shape
    fc = _tile(MOE_FF_CHUNK, f)
    nc = f // fc
    n_tiles = p_rows // tm

    def chunk_of(t, c, tv):
        return jnp.where(tv[t] == 1, c, nc - 1)

    grid_spec = pltpu.PrefetchScalarGridSpec(
        num_scalar_prefetch=3,
        grid=(n_tiles, nc),
        in_specs=[
            pl.BlockSpec((tm, dg), lambda t, c, te, tv, ts: (ts[t], 0)),
            pl.BlockSpec((None, d, fc), lambda t, c, te, tv, ts: (te[t], 0, chunk_of(t, c, tv))),
            pl.BlockSpec((None, d, fc), lambda t, c, te, tv, ts: (te[t], 0, chunk_of(t, c, tv))),
            pl.BlockSpec((None, fc, d), lambda t, c, te, tv, ts: (te[t], chunk_of(t, c, tv), 0)),
        ],
        out_specs=pl.BlockSpec((tm, d), lambda t, c, te, tv, ts: (t, 0)),
        scratch_shapes=[pltpu.VMEM((tm, d), BF16), pltpu.VMEM((tm, d), F32)],
    )
    return pl.pallas_call(
        _moe_body,
        grid_spec=grid_spec,
        out_shape=jax.ShapeDtypeStruct((p_rows, d), F32),
        compiler_params=_cparams("arbitrary", "arbitrary"),
        name="moe",
    )(tile_expert, tile_valid, tile_src, xs, w_gate_e, w_up_e, w_down_e)


def _moe_layer(x, g, w_router, w_gate_e, w_up_e, w_down_e):
    n, d = x.shape
    n_e = w_router.shape[1]
    tm = min(MOE_TILE, n)
    idx, gates, rank, counts = _router(x, g, w_router.T)

    counts = counts[:, 0]
    padded = ((counts + tm - 1) // tm) * tm
    ends = jnp.cumsum(padded)
    offs = ends - padded
    off_of = jnp.zeros_like(idx)
    for e in range(n_e):
        off_of = jnp.where(idx == e, offs[e], off_of)
    dest = off_of + rank
    n_tiles = (TOP_K * n) // tm + n_e
    p_rows = n_tiles * tm
    tile_start = jnp.arange(n_tiles, dtype=jnp.int32) * tm
    tile_valid = (tile_start < ends[-1]).astype(jnp.int32)
    last_valid = jnp.maximum(jnp.sum(tile_valid) - 1, 0)
    tile_src = jnp.minimum(jnp.arange(n_tiles, dtype=jnp.int32), last_valid)
    tile_expert = jnp.minimum(
        jnp.sum((tile_src * tm)[:, None] >= ends[None, :], axis=1), n_e - 1).astype(jnp.int32)

    td = _tile(DISPATCH_TILE, n)
    dest_tiles = dest.reshape(TOP_K, n // td, td).transpose(1, 0, 2).reshape(n // td, 1, TOP_K * td)
    xs = _dispatch(x, g, gates.T, dest_tiles, p_rows, td)
    ys = _moe(xs, tile_expert, tile_valid, tile_src, w_gate_e, w_up_e, w_down_e, tm)
    return x + (jnp.take(ys, dest[0], axis=0) + jnp.take(ys, dest[1], axis=0))


def kernel(x_prompt, x_sample, cache_k, cache_v, state_conv, page_table, norm_attn, w_qkv, q_norm, k_norm, lambda_q1, lambda_k1, lambda_q2, lambda_k2, subln_g, w_o, norm_ffn0, w_gate0, w_up0, w_down0, norm_conv, w_pw1, b_pw1, w_dw, b_dw, ln_g, ln_b, w_pw2, b_pw2, norm_ffn1, w_router, w_gate_e, w_up_e, w_down_e):
    batch, seq, d = x_prompt.shape
    bd, s_new, _ = x_sample.shape
    n_heads = cache_k.shape[2]
    head_dim = q_norm.shape[0]
    hw = 2 * head_dim
    n_p = batch * seq
    n_s = bd * s_new

    row = lambda v: v.reshape(1, -1)
    bf = lambda w: w.astype(BF16)
    xp = x_prompt.reshape(n_p, d)
    xs = x_sample.reshape(n_s, d)
    grp = jnp.kron(jnp.eye(d // head_dim, dtype=F32), jnp.ones((head_dim, head_dim), F32)).astype(BF16)
    qn_t = row(jnp.tile(q_norm, d // head_dim))
    kn_t = row(jnp.tile(k_norm, d // head_dim))
    lam_vecs = (row(lambda_q1), row(lambda_k1), row(lambda_q2), row(lambda_k2))
    sg = row(subln_g)
    w_qkv_bf = bf(w_qkv)

    qk_scale = head_dim ** -0.5
    qp, kp, vp, kpb, vpb = _qkv(xp, row(norm_attn), w_qkv_bf, qn_t, kn_t, grp, head_dim, BF16,
                                qk_scale * math.log2(math.e))
    qs, ks, vs, _, _ = _qkv(xs, row(norm_attn), w_qkv_bf, qn_t, kn_t, grp, head_dim, F32, qk_scale)
    op = _attn_prompt(qp, kpb, vpb, lam_vecs, sg, batch, seq, n_heads, head_dim)
    osm = _attn_sample(qs.reshape(bd, s_new, d), ks.reshape(bd, s_new, d), vs.reshape(bd, s_new, d),
                       cache_k, cache_v, page_table, lam_vecs, sg, n_heads, head_dim)

    ffn0_w = (bf(w_o), row(norm_ffn0), bf(w_gate0), bf(w_up0), bf(w_down0))
    xp = _ffn0(xp, op, *ffn0_w)
    xs = _ffn0(xs, osm.reshape(n_s, d), *ffn0_w)

    conv_w = (row(norm_conv), bf(w_pw1), row(b_pw1), w_dw, row(b_dw), row(ln_g), row(ln_b),
              bf(w_pw2), row(b_pw2))
    xp, conv_prompt = _conv_prompt(xp, batch, seq, *conv_w)
    xs, conv_sample = _conv_sample(xs, state_conv, s_new, *conv_w)

    x_all = _moe_layer(jnp.concatenate([xp, xs], axis=0), row(norm_ffn1), w_router,
                       w_gate_e, w_up_e, w_down_e)

    return (x_all[:n_p].reshape(batch, seq, d), x_all[n_p:].reshape(bd, s_new, d),
            kp.reshape(batch, seq, n_heads, hw), vp.reshape(batch, seq, n_heads, hw),
            ks.reshape(bd, s_new, n_heads, hw), vs.reshape(bd, s_new, n_heads, hw),
            conv_prompt, conv_sample)
```

```python
import functools
import math

import jax
import jax.numpy as jnp
from jax import lax
from jax.experimental import pallas as pl
from jax.experimental.pallas import tpu as pltpu

F32 = jnp.float32
BF16 = jnp.bfloat16

EPS = 1e-6
LAMBDA_INIT = 0.8 - 0.6 * math.exp(-0.3 * 0)
TOP_K = 2
NEG = -1e30
SUBLANES = 8
LANES = 128

VMEM_LIMIT_BYTES = 56 * 1024 * 1024

ROW_TILE_QKV = 256
ROW_TILE_FFN = 512
ATTN_TILE = 512
ATTN_HEADS_PER_STEP = 8
ATTN_ROW_CHUNK = 512
CONV_TILE = 512
CONV_HALO = 32
CONV_ROWS = 16
CONV_SEQ_BLOCK = 32
ROUTER_TILE = 512
MOE_TILE = 1024
MOE_FF_CHUNK = 512
DISPATCH_TILE = 256
GATE_LANES = 128


def _cparams(*sem):
    return pltpu.CompilerParams(dimension_semantics=sem, vmem_limit_bytes=VMEM_LIMIT_BYTES)


def _const_spec(shape):
    nd = len(shape)
    return pl.BlockSpec(shape, lambda *_: (0,) * nd, pipeline_mode=pl.Buffered(1))


def _rms(x, g):
    return x * lax.rsqrt(jnp.mean(x * x, axis=-1, keepdims=True) + EPS) * g


def _silu(x):
    return x * jax.nn.sigmoid(x)


def _tile(limit, n):
    t = min(limit, n)
    assert n % t == 0, (n, t)
    return t


def _div_pow2(x, c):
    assert c > 0 and c & (c - 1) == 0, c
    return lax.shift_right_logical(x, c.bit_length() - 1)


def _mod_pow2(x, c):
    assert c > 0 and c & (c - 1) == 0, c
    return lax.bitwise_and(x, c - 1)


def _qkv_body(x_ref, g_ref, w_ref, qn_ref, kn_ref, grp_ref, q_out, k_out, v_out, kb_out, vb_out,
              *, d_model, head_dim, q_scale):
    x = x_ref[...]
    h = _rms(x, g_ref[...]).astype(BF16)
    qkv = jnp.dot(h, w_ref[...], preferred_element_type=F32)
    q = qkv[:, :d_model]
    k = qkv[:, d_model:2 * d_model]
    v = qkv[:, 2 * d_model:]
    grp = grp_ref[...]

    def head_norm(t, gain):
        ssq = jnp.dot((t * t).astype(BF16), grp, preferred_element_type=F32)
        return t * lax.rsqrt(ssq * (1.0 / head_dim) + EPS) * gain

    qn = head_norm(q, qn_ref[...]) * q_scale
    kn = head_norm(k, kn_ref[...])
    q_out[...] = qn.astype(q_out.dtype)
    k_out[...] = kn
    v_out[...] = v
    kb_out[...] = kn.astype(BF16)
    vb_out[...] = v.astype(BF16)


def _qkv(x, g, w_bf, qn_t, kn_t, grp, head_dim, q_dtype, q_scale):
    n, d = x.shape
    tm = _tile(ROW_TILE_QKV, n)
    row = lambda i: (i, 0)
    out = pl.pallas_call(
        functools.partial(_qkv_body, d_model=d, head_dim=head_dim, q_scale=q_scale),
        grid=(n // tm,),
        in_specs=[
            pl.BlockSpec((tm, d), row),
            _const_spec((1, d)),
            _const_spec((d, 3 * d)),
            _const_spec((1, d)),
            _const_spec((1, d)),
            _const_spec((d, d)),
        ],
        out_specs=[pl.BlockSpec((tm, d), row)] * 5,
        out_shape=[
            jax.ShapeDtypeStruct((n, d), q_dtype),
            jax.ShapeDtypeStruct((n, d), F32),
            jax.ShapeDtypeStruct((n, d), F32),
            jax.ShapeDtypeStruct((n, d), BF16),
            jax.ShapeDtypeStruct((n, d), BF16),
        ],
        compiler_params=_cparams("parallel"),
        name="qkv",
    )(x, g, w_bf, qn_t, kn_t, grp)
    return out


def _lambda(lq1, lk1, lq2, lk2):
    a = jnp.sum(lq1[...] * lk1[...], axis=-1, keepdims=True)
    b = jnp.sum(lq2[...] * lk2[...], axis=-1, keepdims=True)
    return jnp.exp(a) - jnp.exp(b) + LAMBDA_INIT


def _attn_p_body(qi_tab, ki_tab, q_ref, k_ref, v_ref, lq1, lk1, lq2, lk2, sg_ref, o_ref,
                 qq_sc, m_sc, l_sc, acc_sc, *, tq, head_dim, heads):
    step = pl.program_id(2)
    qi = qi_tab[step]
    ki = ki_tab[step]
    hw = 2 * head_dim
    rc = _tile(ATTN_ROW_CHUNK, tq)

    @pl.when(ki == 0)
    def _():
        for h in range(heads):
            q = q_ref[:, h * hw:(h + 1) * hw]
            lane = lax.broadcasted_iota(jnp.int32, q.shape, 1)
            zero = jnp.zeros_like(q)
            qq_sc[h, 0:tq, :] = jnp.where(lane < head_dim, q, zero)
            qq_sc[h, tq:2 * tq, :] = jnp.where(lane >= head_dim, q, zero)
        m_sc[...] = jnp.full(m_sc.shape, NEG, F32)
        l_sc[...] = jnp.zeros(l_sc.shape, F32)
        acc_sc[...] = jnp.zeros(acc_sc.shape, F32)

    def update(diagonal):
        for h in range(heads):
            lanes = slice(h * hw, (h + 1) * hw)
            for r0 in range(0, 2 * tq, rc):
                q0 = r0 % tq
                nk = q0 + rc if diagonal else tq
                rows = slice(r0, r0 + rc)
                s = lax.dot_general(qq_sc[h, rows, :], k_ref[0:nk, lanes], (((1,), (1,)), ((), ())),
                                    preferred_element_type=F32)
                if diagonal:
                    row = lax.broadcasted_iota(jnp.int32, s.shape, 0) + q0
                    col = lax.broadcasted_iota(jnp.int32, s.shape, 1)
                    s = jnp.where(col <= row, s, NEG)
                m_prev = m_sc[h, rows, :]
                m_new = jnp.maximum(m_prev, jnp.max(s, axis=-1, keepdims=True))
                alpha = jnp.exp2(m_prev - m_new)
                p = jnp.exp2(s - jnp.concatenate([m_new] * (nk // hw), axis=1)).astype(BF16)
                v1 = jnp.concatenate([v_ref[0:nk, lanes], jnp.ones((nk, hw), BF16)], axis=1)
                pv = jnp.dot(p, v1, preferred_element_type=F32)
                l_sc[h, rows, :] = alpha * l_sc[h, rows, :] + pv[:, hw:]
                acc_sc[h, rows, :] = alpha * acc_sc[h, rows, :] + pv[:, :hw]
                m_sc[h, rows, :] = m_new

    @pl.when(ki < qi)
    def _():
        update(False)

    @pl.when(ki == qi)
    def _():
        update(True)
        lam = _lambda(lq1, lk1, lq2, lk2)
        for h in range(heads):
            o1 = acc_sc[h, 0:tq, :] / l_sc[h, 0:tq, :]
            o2 = acc_sc[h, tq:2 * tq, :] / l_sc[h, tq:2 * tq, :]
            o = o1 - lam * o2
            o_ref[:, h * hw:(h + 1) * hw] = (_rms(o, sg_ref[...]) * (1.0 - LAMBDA_INIT)).astype(o_ref.dtype)


def _attn_prompt(q_bf, k_bf, v_bf, lam_vecs, subln_g, batch, seq, n_heads, head_dim):
    n, d = q_bf.shape
    hw = 2 * head_dim
    tq = _tile(ATTN_TILE, seq)
    nq = seq // tq
    heads = _tile(ATTN_HEADS_PER_STEP, n_heads)
    pairs = [(i, j) for i in range(nq) for j in range(i + 1)]
    qi_tab = jnp.asarray([p[0] for p in pairs], jnp.int32)
    ki_tab = jnp.asarray([p[1] for p in pairs], jnp.int32)
    q_map = lambda b, h, s, qt, kt: (b * nq + qt[s], h)
    kv_map = lambda b, h, s, qt, kt: (b * nq + kt[s], h)
    vec = pl.BlockSpec((1, head_dim), lambda b, h, s, qt, kt: (0, 0))
    grid_spec = pltpu.PrefetchScalarGridSpec(
        num_scalar_prefetch=2,
        grid=(batch, n_heads // heads, len(pairs)),
        in_specs=[
            pl.BlockSpec((tq, heads * hw), q_map),
            pl.BlockSpec((tq, heads * hw), kv_map),
            pl.BlockSpec((tq, heads * hw), kv_map),
            vec, vec, vec, vec,
            pl.BlockSpec((1, hw), lambda b, h, s, qt, kt: (0, 0)),
        ],
        out_specs=pl.BlockSpec((tq, heads * hw), q_map),
        scratch_shapes=[
            pltpu.VMEM((heads, 2 * tq, hw), BF16),
            pltpu.VMEM((heads, 2 * tq, hw), F32),
            pltpu.VMEM((heads, 2 * tq, hw), F32),
            pltpu.VMEM((heads, 2 * tq, hw), F32),
        ],
    )
    return pl.pallas_call(
        functools.partial(_attn_p_body, tq=tq, head_dim=head_dim, heads=heads),
        grid_spec=grid_spec,
        out_shape=jax.ShapeDtypeStruct((n, d), BF16),
        compiler_params=_cparams("parallel", "parallel", "arbitrary"),
        name="attn_p",
    )(qi_tab, ki_tab, q_bf, k_bf, v_bf, *lam_vecs, subln_g)


def _sample_attention(q_ref, kn_ref, vn_ref, bias_ref, lam, sg, k_refs, v_refs, o_ref,
                      *, n_heads, head_dim, s_new):
    hw = 2 * head_dim
    per_head = 2 * s_new
    rows = n_heads * per_head
    nt = (((1,), (1,)), ((), ()))

    q = q_ref[...]
    r8 = lax.broadcasted_iota(jnp.int32, (per_head, hw), 0)
    l8 = lax.broadcasted_iota(jnp.int32, (per_head, hw), 1)
    own_map = _div_pow2(l8, head_dim) == _div_pow2(r8, s_new)
    blocks = []
    for h in range(n_heads):
        qh = q[:, h * hw:(h + 1) * hw]
        blk = jnp.zeros((per_head, hw), F32)
        for t in range(s_new):
            blk = jnp.where(own_map & (_mod_pow2(r8, s_new) == t), qh[t:t + 1, :], blk)
        blocks.append(blk)
    qm = jnp.concatenate(blocks, axis=0).astype(BF16)

    n_new = kn_ref.shape[0]
    s_own = lax.dot_general(qm, kn_ref[...].astype(BF16), nt, preferred_element_type=F32)
    ridx = lax.broadcasted_iota(jnp.int32, (rows, n_new), 0)
    col = lax.broadcasted_iota(jnp.int32, (rows, n_new), 1)
    ok = (_mod_pow2(col, n_heads) == _div_pow2(ridx, per_head)) & (
        _div_pow2(col, n_heads) <= _mod_pow2(ridx, s_new))
    s_own = jnp.where(ok, s_own, NEG)

    bias = bias_ref[...]
    scores = [lax.dot_general(qm, k_ref[...].astype(BF16), nt, preferred_element_type=F32) + bias
              for k_ref in k_refs]
    m = jnp.max(s_own, axis=-1, keepdims=True)
    for s in scores:
        m = jnp.maximum(m, jnp.max(s, axis=-1, keepdims=True))
    p = jnp.exp(s_own - m)
    l = jnp.sum(p, axis=-1, keepdims=True)
    acc = jnp.dot(p.astype(BF16), vn_ref[...].astype(BF16), preferred_element_type=F32)
    for s, v_ref in zip(scores, v_refs):
        p = jnp.exp(s - m)
        l = l + jnp.sum(p, axis=-1, keepdims=True)
        acc = acc + jnp.dot(p.astype(BF16), v_ref[...].astype(BF16), preferred_element_type=F32)

    accn = acc / l
    for h in range(n_heads):
        blk = accn[h * per_head:(h + 1) * per_head]
        oh = blk[0:s_new] - lam * blk[s_new:per_head]
        o_ref[:, h * hw:(h + 1) * hw] = _rms(oh, sg) * (1.0 - LAMBDA_INIT)


def _qkv_attn_s_body(pt_ref, x_ref, g_ref, w_ref, qn_ref, kn_ref, grp_ref,
                     qs_ref, ksn_ref, vsn_ref, bias_ref, lq1, lk1, lq2, lk2, sg_ref, *rest,
                     n_pages, d_model, n_heads, head_dim, s_new, q_scale):
    del pt_ref
    k_refs = rest[:n_pages]
    v_refs = rest[n_pages:2 * n_pages]
    q_out, k_out, v_out, kb_out, vb_out, o_ref = rest[2 * n_pages:]
    _qkv_body(x_ref, g_ref, w_ref, qn_ref, kn_ref, grp_ref, q_out, k_out, v_out, kb_out, vb_out,
              d_model=d_model, head_dim=head_dim, q_scale=q_scale)
    _sample_attention(qs_ref, ksn_ref, vsn_ref, bias_ref, _lambda(lq1, lk1, lq2, lk2), sg_ref[...],
                      k_refs, v_refs, o_ref, n_heads=n_heads, head_dim=head_dim, s_new=s_new)


def _qkv_attn_sample(x, g, w_bf, qn_t, kn_t, grp, q_scale, q_s, k_new, v_new, cache_k, cache_v,
                     page_table, lam_vecs, subln_g, n_heads, head_dim):
    n, d = x.shape
    bd, s_new, _ = q_s.shape
    n_phys, page, _, hw = cache_k.shape
    prow = page * n_heads
    ck = cache_k.reshape(n_phys, prow, hw)
    cv = cache_v.reshape(n_phys, prow, hw)
    n_pages = page_table.shape[1]
    assert n % bd == 0 and (n // bd) % SUBLANES == 0, (n, bd)
    tm = n // bd
    rows = n_heads * 2 * s_new
    ridx = jnp.arange(rows, dtype=jnp.int32)[:, None] // (2 * s_new)
    col = jnp.arange(prow, dtype=jnp.int32)[None, :] % n_heads
    bias = jnp.where(col == ridx, 0.0, NEG).astype(F32)

    row = pl.BlockSpec((tm, d), lambda b, pt: (b, 0))
    const = lambda shape: pl.BlockSpec(shape, lambda b, pt: (0,) * len(shape),
                                       pipeline_mode=pl.Buffered(1))
    seq_spec = pl.BlockSpec((None, s_new, d), lambda b, pt: (b, 0, 0))
    new_spec = pl.BlockSpec((None, s_new * n_heads, hw), lambda b, pt: (b, 0, 0))
    vec = const((1, head_dim))

    def page_spec(i):
        return pl.BlockSpec((None, prow, hw), lambda b, pt: (pt[b, i], 0, 0))

    grid_spec = pltpu.PrefetchScalarGridSpec(
        num_scalar_prefetch=1,
        grid=(bd,),
        in_specs=[row, const((1, d)), const((d, 3 * d)), const((1, d)), const((1, d)), const((d, d)),
                  seq_spec, new_spec, new_spec, const((rows, prow)), vec, vec, vec, vec,
                  const((1, hw))]
                 + [page_spec(i) for i in range(n_pages)] + [page_spec(i) for i in range(n_pages)],
        out_specs=[row] * 5 + [seq_spec],
    )
    return pl.pallas_call(
        functools.partial(_qkv_attn_s_body, n_pages=n_pages, d_model=d, n_heads=n_heads,
                          head_dim=head_dim, s_new=s_new, q_scale=q_scale),
        grid_spec=grid_spec,
        out_shape=[
            jax.ShapeDtypeStruct((n, d), BF16),
            jax.ShapeDtypeStruct((n, d), F32),
            jax.ShapeDtypeStruct((n, d), F32),
            jax.ShapeDtypeStruct((n, d), BF16),
            jax.ShapeDtypeStruct((n, d), BF16),
            jax.ShapeDtypeStruct((bd, s_new, d), F32),
        ],
        compiler_params=_cparams("parallel"),
        name="qkv_attn_s",
    )(page_table, x, g, w_bf, qn_t, kn_t, grp,
      q_s, k_new.reshape(bd, s_new * n_heads, hw), v_new.reshape(bd, s_new * n_heads, hw), bias,
      *lam_vecs, subln_g, *([ck] * n_pages), *([cv] * n_pages))


def _ffn0_body(x_ref, o_ref, wo_ref, g_ref, wg_ref, wu_ref, wd_ref, y_ref, *, ff_chunks):
    x1 = x_ref[...] + jnp.dot(o_ref[...].astype(BF16), wo_ref[...], preferred_element_type=F32)
    h = _rms(x1, g_ref[...]).astype(BF16)
    y = x1
    for lo, hi in ff_chunks:
        gate = jnp.dot(h, wg_ref[:, lo:hi], preferred_element_type=F32)
        up = jnp.dot(h, wu_ref[:, lo:hi], preferred_element_type=F32)
        a = (_silu(gate) * up).astype(BF16)
        y = y + jnp.dot(a, wd_ref[lo:hi, :], preferred_element_type=F32)
    y_ref[...] = y


def _ffn0(x, o, wo_bf, g, wg_bf, wu_bf, wd_bf):
    n, d = x.shape
    f = wg_bf.shape[1]
    tm = _tile(ROW_TILE_FFN, n)
    half = (f // 256) * 128
    ff_chunks = ((0, half), (half, f))
    row = lambda i: (i, 0)
    return pl.pallas_call(
        functools.partial(_ffn0_body, ff_chunks=ff_chunks),
        grid=(n // tm,),
        in_specs=[
            pl.BlockSpec((tm, d), row),
            pl.BlockSpec((tm, d), row),
            _const_spec((d, d)),
            _const_spec((1, d)),
            _const_spec((d, f)),
            _const_spec((d, f)),
            _const_spec((f, d)),
        ],
        out_specs=pl.BlockSpec((tm, d), row),
        out_shape=jax.ShapeDtypeStruct((n, d), F32),
        compiler_params=_cparams("parallel"),
        name="ffn0",
    )(x, o, wo_bf, g, wg_bf, wu_bf, wd_bf)


def _glu_pointwise(x, g_ref, w1_ref, b1_ref, d):
    h = _rms(x, g_ref[...]).astype(BF16)
    u = jnp.dot(h, w1_ref[...], preferred_element_type=F32) + b1_ref[...]
    return u[:, :d] * jax.nn.sigmoid(u[:, d:])


def _ln_swish_pointwise(c, lng_ref, lnb_ref, w2_ref, b2_ref):
    mu = jnp.mean(c, axis=-1, keepdims=True)
    xc = c - mu
    yln = xc * lax.rsqrt(jnp.mean(xc * xc, axis=-1, keepdims=True) + EPS) * lng_ref[...] + lnb_ref[...]
    return jnp.dot(_silu(yln).astype(BF16), w2_ref[...], preferred_element_type=F32) + b2_ref[...]


def _conv_p_body(x_ref, g_ref, w1_ref, b1_ref, wdw_ref, bdw_ref, lng_ref, lnb_ref, w2_ref, b2_ref,
                 y_ref, st_ref, u_sc, sh_sc, c_sc, *, tt, width):
    t = pl.program_id(1)
    d = x_ref.shape[-1]
    keep = width - 1

    @pl.when(t == 0)
    def _():
        u_sc[0:CONV_HALO, :] = jnp.zeros((CONV_HALO, d), F32)

    x = x_ref[...]
    u_sc[CONV_HALO:CONV_HALO + tt, :] = _glu_pointwise(x, g_ref, w1_ref, b1_ref, d)

    base = CONV_HALO - keep
    span = sh_sc.shape[1]
    for b in range(1, SUBLANES):
        sh_sc[b - 1, :, :] = u_sc[b:b + span, :]

    def chunk(r, carry):
        r0 = pl.multiple_of(r * CONV_ROWS, CONV_ROWS)
        for c0 in range(0, d, LANES):
            lanes = slice(c0, c0 + LANES)
            acc = jnp.zeros((CONV_ROWS, LANES), F32)
            for j in range(width):
                a, b = divmod(base + j, SUBLANES)
                start = pl.multiple_of(r0 + SUBLANES * a, SUBLANES)
                if b == 0:
                    tap = u_sc[pl.ds(start, CONV_ROWS), lanes]
                else:
                    tap = sh_sc[b - 1, pl.ds(start, CONV_ROWS), lanes]
                acc = acc + wdw_ref[j:j + 1, lanes] * tap
            c_sc[pl.ds(r0, CONV_ROWS), lanes] = acc
        return carry

    lax.fori_loop(0, tt // CONV_ROWS, chunk, 0)

    c = c_sc[...] + bdw_ref[...]
    y_ref[...] = x + _ln_swish_pointwise(c, lng_ref, lnb_ref, w2_ref, b2_ref)
    st_ref[...] = u_sc[CONV_HALO + tt - keep:CONV_HALO + tt, :]
    u_sc[0:CONV_HALO, :] = u_sc[tt:tt + CONV_HALO, :]


def _conv_prompt(x, batch, seq, g, w1_bf, b1, wdw, bdw, lng, lnb, w2_bf, b2):
    n, d = x.shape
    width = wdw.shape[0]
    tt = _tile(CONV_TILE, seq)
    nt = seq // tt
    row = lambda b, t: (b * nt + t, 0)
    y, st = pl.pallas_call(
        functools.partial(_conv_p_body, tt=tt, width=width),
        grid=(batch, nt),
        in_specs=[
            pl.BlockSpec((tt, d), row),
            _const_spec((1, d)),
            _const_spec((d, 2 * d)),
            _const_spec((1, 2 * d)),
            _const_spec((width, d)),
            _const_spec((1, d)),
            _const_spec((1, d)),
            _const_spec((1, d)),
            _const_spec((d, d)),
            _const_spec((1, d)),
        ],
        out_specs=[
            pl.BlockSpec((tt, d), row),
            pl.BlockSpec((None, width - 1, d), lambda b, t: (b, 0, 0)),
        ],
        out_shape=[
            jax.ShapeDtypeStruct((n, d), F32),
            jax.ShapeDtypeStruct((batch, width - 1, d), F32),
        ],
        scratch_shapes=[
            pltpu.VMEM((CONV_HALO + tt, d), F32),
            pltpu.VMEM((SUBLANES - 1, CONV_HALO + tt - SUBLANES, d), F32),
            pltpu.VMEM((tt, d), F32),
        ],
        compiler_params=_cparams("parallel", "arbitrary"),
        name="conv_p",
    )(x, g, w1_bf, b1, wdw, bdw, lng, lnb, w2_bf, b2)
    return y, st


def _conv_s_body(x_ref, st_in_ref, g_ref, w1_ref, b1_ref, wdw_ref, bdw_ref, lng_ref, lnb_ref,
                 w2_ref, b2_ref, y_ref, st_out_ref, ext_sc, c_sc, *, s_new, width, n_seq):
    d = x_ref.shape[-1]
    keep = width - 1
    x = x_ref[...]
    ug = _glu_pointwise(x, g_ref, w1_ref, b1_ref, d)
    ext_sc[:, 0:keep, :] = st_in_ref[...]
    for s in range(n_seq):
        ext_sc[s, keep:keep + s_new, :] = ug[s * s_new:(s + 1) * s_new, :]
    acc = jnp.zeros((n_seq, s_new, d), F32)
    for j in range(width):
        acc = acc + wdw_ref[j:j + 1, :][None] * ext_sc[:, j:j + s_new, :]
    for s in range(n_seq):
        c_sc[s * s_new:(s + 1) * s_new, :] = acc[s]
    st_out_ref[...] = ext_sc[:, s_new:s_new + keep, :]
    c = c_sc[...] + bdw_ref[...]
    y_ref[...] = x + _ln_swish_pointwise(c, lng_ref, lnb_ref, w2_ref, b2_ref)


def _conv_sample(x, state, s_new, g, w1_bf, b1, wdw, bdw, lng, lnb, w2_bf, b2):
    n, d = x.shape
    bd = state.shape[0]
    width = wdw.shape[0]
    keep = width - 1
    ns = _tile(CONV_SEQ_BLOCK, bd)
    y, st = pl.pallas_call(
        functools.partial(_conv_s_body, s_new=s_new, width=width, n_seq=ns),
        grid=(bd // ns,),
        in_specs=[
            pl.BlockSpec((ns * s_new, d), lambda i: (i, 0)),
            pl.BlockSpec((ns, keep, d), lambda i: (i, 0, 0)),
            _const_spec((1, d)),
            _const_spec((d, 2 * d)),
            _const_spec((1, 2 * d)),
            _const_spec((width, d)),
            _const_spec((1, d)),
            _const_spec((1, d)),
            _const_spec((1, d)),
            _const_spec((d, d)),
            _const_spec((1, d)),
        ],
        out_specs=[
            pl.BlockSpec((ns * s_new, d), lambda i: (i, 0)),
            pl.BlockSpec((ns, keep, d), lambda i: (i, 0, 0)),
        ],
        out_shape=[
            jax.ShapeDtypeStruct((n, d), F32),
            jax.ShapeDtypeStruct((bd, keep, d), F32),
        ],
        scratch_shapes=[
            pltpu.VMEM((ns, keep + s_new, d), F32),
            pltpu.VMEM((ns * s_new, d), F32),
        ],
        compiler_params=_cparams("parallel"),
        name="conv_s",
    )(x, state, g, w1_bf, b1, wdw, bdw, lng, lnb, w2_bf, b2)
    return y, st


def _router_body(x_ref, g_ref, wrt_ref, tri_ref, idx_ref, gate_ref, rank_ref, cnt_ref, cnt_sc):
    i = pl.program_id(0)

    @pl.when(i == 0)
    def _():
        cnt_sc[...] = jnp.zeros(cnt_sc.shape, F32)

    h = _rms(x_ref[...], g_ref[...])
    logits = lax.dot_general(wrt_ref[...], h, (((1,), (1,)), ((), ())),
                             precision=lax.Precision.HIGHEST,
                             preferred_element_type=F32)
    n_e = float(logits.shape[0])
    eidx = lax.broadcasted_iota(jnp.int32, logits.shape, 0).astype(F32)
    v1 = jnp.max(logits, axis=0, keepdims=True)
    i1 = jnp.min(jnp.where(logits == v1, eidx, n_e), axis=0, keepdims=True)
    rest = jnp.where(eidx == i1, -jnp.inf, logits)
    v2 = jnp.max(rest, axis=0, keepdims=True)
    i2 = jnp.min(jnp.where(rest == v2, eidx, n_e), axis=0, keepdims=True)
    e = jnp.exp(v2 - v1)
    g1 = 1.0 / (1.0 + e)
    idx_ref[0:1, :] = i1.astype(jnp.int32)
    idx_ref[1:2, :] = i2.astype(jnp.int32)
    gate_ref[0:1, :] = g1
    gate_ref[1:2, :] = e * g1

    hot1 = (eidx == i1).astype(F32)
    hot2 = (eidx == i2).astype(F32)
    tri = tri_ref[...]
    incl1 = jnp.dot(hot1.astype(BF16), tri, preferred_element_type=F32)
    incl2 = jnp.dot(hot2.astype(BF16), tri, preferred_element_type=F32)
    cnt = cnt_sc[...]
    c1 = jnp.sum(hot1, axis=1, keepdims=True)
    c2 = jnp.sum(hot2, axis=1, keepdims=True)
    r1 = jnp.sum(hot1 * (cnt + incl1), axis=0, keepdims=True) - 1.0
    r2 = jnp.sum(hot2 * (cnt + c1 + incl2), axis=0, keepdims=True) - 1.0
    rank_ref[0:1, :] = r1.astype(jnp.int32)
    rank_ref[1:2, :] = r2.astype(jnp.int32)
    cnt_new = cnt + c1 + c2
    cnt_sc[...] = cnt_new
    cnt_ref[...] = cnt_new.astype(jnp.int32)


def _router(x, g, w_router_t):
    n, d = x.shape
    n_e = w_router_t.shape[0]
    tm = _tile(ROUTER_TILE, n)
    tri = (jnp.arange(tm)[:, None] <= jnp.arange(tm)[None, :]).astype(BF16)
    tok = pl.BlockSpec((TOP_K, tm), lambda i: (0, i))
    return pl.pallas_call(
        _router_body,
        grid=(n // tm,),
        in_specs=[
            pl.BlockSpec((tm, d), lambda i: (i, 0)),
            _const_spec((1, d)),
            _const_spec((n_e, d)),
            _const_spec((tm, tm)),
        ],
        out_specs=[tok, tok, tok, pl.BlockSpec((n_e, 1), lambda i: (0, 0))],
        out_shape=[
            jax.ShapeDtypeStruct((TOP_K, n), jnp.int32),
            jax.ShapeDtypeStruct((TOP_K, n), F32),
            jax.ShapeDtypeStruct((TOP_K, n), jnp.int32),
            jax.ShapeDtypeStruct((n_e, 1), jnp.int32),
        ],
        scratch_shapes=[pltpu.VMEM((n_e, 1), F32)],
        compiler_params=_cparams("arbitrary"),
        name="router",
    )(x, g, w_router_t, tri)


def _dispatch_body(dest_ref, x_ref, g_ref, gate_ref, xs_in, xs_out, buf_sc, sem, *, tm, d):
    del xs_in
    i = pl.program_id(0)
    slot = lax.rem(i, 2)
    h = _rms(x_ref[...], g_ref[...])
    gate = gate_ref[...]
    for k in range(TOP_K):
        buf_sc[slot, k, :, 0:d] = h
        buf_sc[slot, k, :, d:] = jnp.broadcast_to(gate[:, k:k + 1], (tm, GATE_LANES))

    def issue(t, carry):
        for k in range(TOP_K):
            dst = dest_ref[0, k * tm + t]
            pltpu.make_async_copy(buf_sc.at[slot, k, pl.ds(t, 1), :],
                                  xs_out.at[pl.ds(dst, 1), :], sem.at[slot]).start()
        return carry

    lax.fori_loop(0, tm, issue, 0, unroll=8)

    def wait_rows(which):
        for k in range(TOP_K):
            pltpu.make_async_copy(buf_sc.at[which, k], xs_out.at[pl.ds(0, tm), :],
                                  sem.at[which]).wait()

    @pl.when(i > 0)
    def _():
        wait_rows(1 - slot)

    @pl.when(i == pl.num_programs(0) - 1)
    def _():
        wait_rows(slot)


def _dispatch(x, g, gate_cols, dest_tiles, p_rows, tm):
    n, d = x.shape
    dg = d + GATE_LANES
    xs0 = jnp.zeros((p_rows, dg), F32)
    return pl.pallas_call(
        functools.partial(_dispatch_body, tm=tm, d=d),
        grid=(n // tm,),
        in_specs=[
            pl.BlockSpec((None, 1, TOP_K * tm), lambda i: (i, 0, 0), memory_space=pltpu.SMEM),
            pl.BlockSpec((tm, d), lambda i: (i, 0)),
            _const_spec((1, d)),
            pl.BlockSpec((tm, TOP_K), lambda i: (i, 0)),
            pl.BlockSpec(memory_space=pl.ANY),
        ],
        out_specs=pl.BlockSpec(memory_space=pl.ANY),
        out_shape=jax.ShapeDtypeStruct((p_rows, dg), F32),
        scratch_shapes=[
            pltpu.VMEM((2, TOP_K, tm, dg), F32),
            pltpu.SemaphoreType.DMA((2,)),
        ],
        input_output_aliases={4: 0},
        compiler_params=_cparams("arbitrary"),
        name="dispatch",
    )(dest_tiles, x, g, gate_cols, xs0)


def _moe_body(te_ref, tv_ref, ts_ref, x_ref, wg_ref, wu_ref, wd_ref, y_ref, xb_sc, acc_sc):
    t = pl.program_id(0)
    c = pl.program_id(1)
    d = acc_sc.shape[1]

    @pl.when(tv_ref[t] == 1)
    def _():
        @pl.when(c == 0)
        def _():
            acc_sc[...] = jnp.zeros(acc_sc.shape, F32)
            xb_sc[...] = x_ref[:, 0:d].astype(BF16)

        x = xb_sc[...]
        gate = jnp.dot(x, wg_ref[...].astype(BF16), preferred_element_type=F32)
        up = jnp.dot(x, wu_ref[...].astype(BF16), preferred_element_type=F32)
        a = (_silu(gate) * up).astype(BF16)
        acc_sc[...] += jnp.dot(a, wd_ref[...].astype(BF16), preferred_element_type=F32)

        @pl.when(c == pl.num_programs(1) - 1)
        def _():
            y_ref[...] = acc_sc[...] * x_ref[:, d:d + 1]

    @pl.when((tv_ref[t] == 0) & (c == pl.num_programs(1) - 1))
    def _():
        y_ref[...] = jnp.zeros(y_ref.shape, F32)


def _moe(xs, tile_expert, tile_valid, tile_src, w_gate_e, w_up_e, w_down_e, tm):
    p_rows, dg = xs.shape
    n_e, d, f = w_gate_e.shape
    fc = _tile(MOE_FF_CHUNK, f)
    nc = f // fc
    n_tiles = p_rows // tm

    def chunk_of(t, c, tv):
        return jnp.where(tv[t] == 1, c, nc - 1)

    grid_spec = pltpu.PrefetchScalarGridSpec(
        num_scalar_prefetch=3,
        grid=(n_tiles, nc),
        in_specs=[
            pl.BlockSpec((tm, dg), lambda t, c, te, tv, ts: (ts[t], 0)),
            pl.BlockSpec((None, d, fc), lambda t, c, te, tv, ts: (te[t], 0, chunk_of(t, c, tv))),
            pl.BlockSpec((None, d, fc), lambda t, c, te, tv, ts: (te[t], 0, chunk_of(t, c, tv))),
            pl.BlockSpec((None, fc, d), lambda t, c, te, tv, ts: (te[t], chunk_of(t, c, tv), 0)),
        ],
        out_specs=pl.BlockSpec((tm, d), lambda t, c, te, tv, ts: (t, 0)),
        scratch_shapes=[pltpu.VMEM((tm, d), BF16), pltpu.VMEM((tm, d), F32)],
    )
    return pl.pallas_call(
        _moe_body,
        grid_spec=grid_spec,
        out_shape=jax.ShapeDtypeStruct((p_rows, d), F32),
        compiler_params=_cparams("arbitrary", "arbitrary"),
        name="moe",
    )(tile_expert, tile_valid, tile_src, xs, w_gate_e, w_up_e, w_down_e)


def _moe_layer(x, g, w_router, w_gate_e, w_up_e, w_down_e):
    n, d = x.shape
    n_e = w_router.shape[1]
    tm = min(MOE_TILE, n)
    idx, gates, rank, counts = _router(x, g, w_router.T)

    counts = counts[:, 0]
    padded = ((counts + tm - 1) // tm) * tm
    ends = jnp.cumsum(padded)
    offs = ends - padded
    off_of = jnp.zeros_like(idx)
    for e in range(n_e):
        off_of = jnp.where(idx == e, offs[e], off_of)
    dest = off_of + rank
    n_tiles = (TOP_K * n) // tm + n_e
    p_rows = n_tiles * tm
    tile_start = jnp.arange(n_tiles, dtype=jnp.int32) * tm
    tile_valid = (tile_start < ends[-1]).astype(jnp.int32)
    last_valid = jnp.maximum(jnp.sum(tile_valid) - 1, 0)
    tile_src = jnp.minimum(jnp.arange(n_tiles, dtype=jnp.int32), last_valid)
    tile_expert = jnp.minimum(
        jnp.sum((tile_src * tm)[:, None] >= ends[None, :], axis=1), n_e - 1).astype(jnp.int32)

    td = _tile(DISPATCH_TILE, n)
    dest_tiles = dest.reshape(TOP_K, n // td, td).transpose(1, 0, 2).reshape(n // td, 1, TOP_K * td)
    xs = _dispatch(x, g, gates.T, dest_tiles, p_rows, td)
    ys = _moe(xs, tile_expert, tile_valid, tile_src, w_gate_e, w_up_e, w_down_e, tm)
    return x + (jnp.take(ys, dest[0], axis=0) + jnp.take(ys, dest[1], axis=0))


def kernel(x_prompt, x_sample, cache_k, cache_v, state_conv, page_table, norm_attn, w_qkv, q_norm, k_norm, lambda_q1, lambda_k1, lambda_q2, lambda_k2, subln_g, w_o, norm_ffn0, w_gate0, w_up0, w_down0, norm_conv, w_pw1, b_pw1, w_dw, b_dw, ln_g, ln_b, w_pw2, b_pw2, norm_ffn1, w_router, w_gate_e, w_up_e, w_down_e):
    batch, seq, d = x_prompt.shape
    bd, s_new, _ = x_sample.shape
    n_heads = cache_k.shape[2]
    head_dim = q_norm.shape[0]
    hw = 2 * head_dim
    n_p = batch * seq
    n_s = bd * s_new

    row = lambda v: v.reshape(1, -1)
    bf = lambda w: w.astype(BF16)
    xp = x_prompt.reshape(n_p, d)
    xs = x_sample.reshape(n_s, d)
    grp = jnp.kron(jnp.eye(d // head_dim, dtype=F32), jnp.ones((head_dim, head_dim), F32)).astype(BF16)
    qn_t = row(jnp.tile(q_norm, d // head_dim))
    kn_t = row(jnp.tile(k_norm, d // head_dim))
    lam_vecs = (row(lambda_q1), row(lambda_k1), row(lambda_q2), row(lambda_k2))
    sg = row(subln_g)
    w_qkv_bf = bf(w_qkv)

    qk_scale = head_dim ** -0.5
    qs, ks, vs, _, _ = _qkv(xs, row(norm_attn), w_qkv_bf, qn_t, kn_t, grp, head_dim, F32, qk_scale)
    qp, kp, vp, kpb, vpb, osm = _qkv_attn_sample(
        xp, row(norm_attn), w_qkv_bf, qn_t, kn_t, grp, qk_scale * math.log2(math.e),
        qs.reshape(bd, s_new, d), ks, vs, cache_k, cache_v, page_table, lam_vecs, sg, n_heads, head_dim)
    op = _attn_prompt(qp, kpb, vpb, lam_vecs, sg, batch, seq, n_heads, head_dim)

    ffn0_w = (bf(w_o), row(norm_ffn0), bf(w_gate0), bf(w_up0), bf(w_down0))
    xp = _ffn0(xp, op, *ffn0_w)
    xs = _ffn0(xs, osm.reshape(n_s, d), *ffn0_w)

    conv_w = (row(norm_conv), bf(w_pw1), row(b_pw1), w_dw, row(b_dw), row(ln_g), row(ln_b),
              bf(w_pw2), row(b_pw2))
    xp, conv_prompt = _conv_prompt(xp, batch, seq, *conv_w)
    xs, conv_sample = _conv_sample(xs, state_conv, s_new, *conv_w)

    x_all = _moe_layer(jnp.concatenate([xp, xs], axis=0), row(norm_ffn1), w_router,
                       w_gate_e, w_up_e, w_down_e)

    return (x_all[:n_p].reshape(batch, seq, d), x_all[n_p:].reshape(bd, s_new, d),
            kp.reshape(batch, seq, n_heads, hw), vp.reshape(batch, seq, n_heads, hw),
            ks.reshape(bd, s_new, n_heads, hw), vs.reshape(bd, s_new, n_heads, hw),
            conv_prompt, conv_sample)
```

```python
import functools
import math

import jax
import jax.numpy as jnp
from jax import lax
from jax.experimental import pallas as pl
from jax.experimental.pallas import tpu as pltpu

F32 = jnp.float32
BF16 = jnp.bfloat16

EPS = 1e-6
LAMBDA_INIT = 0.8 - 0.6 * math.exp(-0.3 * 0)
TOP_K = 2
NEG = -1e30
SUBLANES = 8
LANES = 128

VMEM_LIMIT_BYTES = 56 * 1024 * 1024

ROW_TILE_QKV = 256
ROW_TILE_FFN = 512
ATTN_TILE = 512
ATTN_HEADS_PER_STEP = 8
ATTN_ROW_CHUNK = 512
CONV_TILE = 512
CONV_HALO = 32
CONV_ROWS = 16
CONV_SEQ_BLOCK = 32
ROUTER_TILE = 512
MOE_TILE = 1024
MOE_FF_CHUNK = 512
DISPATCH_TILE = 256
GATE_LANES = 128


def _cparams(*sem):
    return pltpu.CompilerParams(dimension_semantics=sem, vmem_limit_bytes=VMEM_LIMIT_BYTES)


def _const_spec(shape):
    nd = len(shape)
    return pl.BlockSpec(shape, lambda *_: (0,) * nd, pipeline_mode=pl.Buffered(1))


def _rms(x, g):
    return x * lax.rsqrt(jnp.mean(x * x, axis=-1, keepdims=True) + EPS) * g


def _silu(x):
    return x * jax.nn.sigmoid(x)


def _tile(limit, n):
    t = min(limit, n)
    assert n % t == 0, (n, t)
    return t


def _div_pow2(x, c):
    assert c > 0 and c & (c - 1) == 0, c
    return lax.shift_right_logical(x, c.bit_length() - 1)


def _mod_pow2(x, c):
    assert c > 0 and c & (c - 1) == 0, c
    return lax.bitwise_and(x, c - 1)


def _qkv_body(x_ref, g_ref, w_ref, qn_ref, kn_ref, grp_ref, q_out, k_out, v_out, kb_out, vb_out,
              *, d_model, head_dim, q_scale):
    x = x_ref[...]
    h = _rms(x, g_ref[...]).astype(BF16)
    qkv = jnp.dot(h, w_ref[...], preferred_element_type=F32)
    q = qkv[:, :d_model]
    k = qkv[:, d_model:2 * d_model]
    v = qkv[:, 2 * d_model:]
    grp = grp_ref[...]

    def head_norm(t, gain):
        ssq = jnp.dot((t * t).astype(BF16), grp, preferred_element_type=F32)
        return t * lax.rsqrt(ssq * (1.0 / head_dim) + EPS) * gain

    qn = head_norm(q, qn_ref[...]) * q_scale
    kn = head_norm(k, kn_ref[...])
    q_out[...] = qn.astype(q_out.dtype)
    k_out[...] = kn
    v_out[...] = v
    kb_out[...] = kn.astype(BF16)
    vb_out[...] = v.astype(BF16)


def _qkv(x, g, w_bf, qn_t, kn_t, grp, head_dim, q_dtype, q_scale):
    n, d = x.shape
    tm = _tile(ROW_TILE_QKV, n)
    row = lambda i: (i, 0)
    out = pl.pallas_call(
        functools.partial(_qkv_body, d_model=d, head_dim=head_dim, q_scale=q_scale),
        grid=(n // tm,),
        in_specs=[
            pl.BlockSpec((tm, d), row),
            _const_spec((1, d)),
            _const_spec((d, 3 * d)),
            _const_spec((1, d)),
            _const_spec((1, d)),
            _const_spec((d, d)),
        ],
        out_specs=[pl.BlockSpec((tm, d), row)] * 5,
        out_shape=[
            jax.ShapeDtypeStruct((n, d), q_dtype),
            jax.ShapeDtypeStruct((n, d), F32),
            jax.ShapeDtypeStruct((n, d), F32),
            jax.ShapeDtypeStruct((n, d), BF16),
            jax.ShapeDtypeStruct((n, d), BF16),
        ],
        compiler_params=_cparams("parallel"),
        name="qkv",
    )(x, g, w_bf, qn_t, kn_t, grp)
    return out


def _lambda(lq1, lk1, lq2, lk2):
    a = jnp.sum(lq1[...] * lk1[...], axis=-1, keepdims=True)
    b = jnp.sum(lq2[...] * lk2[...], axis=-1, keepdims=True)
    return jnp.exp(a) - jnp.exp(b) + LAMBDA_INIT


def _attn_p_body(qi_tab, ki_tab, q_ref, k_ref, v_ref, lq1, lk1, lq2, lk2, sg_ref, o_ref,
                 qq_sc, m_sc, l_sc, acc_sc, *, tq, head_dim, heads):
    step = pl.program_id(2)
    qi = qi_tab[step]
    ki = ki_tab[step]
    hw = 2 * head_dim
    rc = _tile(ATTN_ROW_CHUNK, tq)

    @pl.when(ki == 0)
    def _():
        for h in range(heads):
            q = q_ref[:, h * hw:(h + 1) * hw]
            lane = lax.broadcasted_iota(jnp.int32, q.shape, 1)
            zero = jnp.zeros_like(q)
            qq_sc[h, 0:tq, :] = jnp.where(lane < head_dim, q, zero)
            qq_sc[h, tq:2 * tq, :] = jnp.where(lane >= head_dim, q, zero)
        m_sc[...] = jnp.full(m_sc.shape, NEG, F32)
        l_sc[...] = jnp.zeros(l_sc.shape, F32)
        acc_sc[...] = jnp.zeros(acc_sc.shape, F32)

    def update(diagonal):
        for h in range(heads):
            lanes = slice(h * hw, (h + 1) * hw)
            for r0 in range(0, 2 * tq, rc):
                q0 = r0 % tq
                nk = q0 + rc if diagonal else tq
                rows = slice(r0, r0 + rc)
                s = lax.dot_general(qq_sc[h, rows, :], k_ref[0:nk, lanes], (((1,), (1,)), ((), ())),
                                    preferred_element_type=F32)
                if diagonal:
                    row = lax.broadcasted_iota(jnp.int32, s.shape, 0) + q0
                    col = lax.broadcasted_iota(jnp.int32, s.shape, 1)
                    s = jnp.where(col <= row, s, NEG)
                m_prev = m_sc[h, rows, :]
                m_new = jnp.maximum(m_prev, jnp.max(s, axis=-1, keepdims=True))
                alpha = jnp.exp2(m_prev - m_new)
                p = jnp.exp2(s - jnp.concatenate([m_new] * (nk // hw), axis=1)).astype(BF16)
                v1 = jnp.concatenate([v_ref[0:nk, lanes], jnp.ones((nk, hw), BF16)], axis=1)
                pv = jnp.dot(p, v1, preferred_element_type=F32)
                l_sc[h, rows, :] = alpha * l_sc[h, rows, :] + pv[:, hw:]
                acc_sc[h, rows, :] = alpha * acc_sc[h, rows, :] + pv[:, :hw]
                m_sc[h, rows, :] = m_new

    @pl.when(ki < qi)
    def _():
        update(False)

    @pl.when(ki == qi)
    def _():
        update(True)
        lam = _lambda(lq1, lk1, lq2, lk2)
        for h in range(heads):
            o1 = acc_sc[h, 0:tq, :] / l_sc[h, 0:tq, :]
            o2 = acc_sc[h, tq:2 * tq, :] / l_sc[h, tq:2 * tq, :]
            o = o1 - lam * o2
            o_ref[:, h * hw:(h + 1) * hw] = (_rms(o, sg_ref[...]) * (1.0 - LAMBDA_INIT)).astype(o_ref.dtype)


def _attn_prompt(q_bf, k_bf, v_bf, lam_vecs, subln_g, batch, seq, n_heads, head_dim):
    n, d = q_bf.shape
    hw = 2 * head_dim
    tq = _tile(ATTN_TILE, seq)
    nq = seq // tq
    heads = _tile(ATTN_HEADS_PER_STEP, n_heads)
    pairs = [(i, j) for i in range(nq) for j in range(i + 1)]
    qi_tab = jnp.asarray([p[0] for p in pairs], jnp.int32)
    ki_tab = jnp.asarray([p[1] for p in pairs], jnp.int32)
    q_map = lambda b, h, s, qt, kt: (b * nq + qt[s], h)
    kv_map = lambda b, h, s, qt, kt: (b * nq + kt[s], h)
    vec = pl.BlockSpec((1, head_dim), lambda b, h, s, qt, kt: (0, 0))
    grid_spec = pltpu.PrefetchScalarGridSpec(
        num_scalar_prefetch=2,
        grid=(batch, n_heads // heads, len(pairs)),
        in_specs=[
            pl.BlockSpec((tq, heads * hw), q_map),
            pl.BlockSpec((tq, heads * hw), kv_map),
            pl.BlockSpec((tq, heads * hw), kv_map),
            vec, vec, vec, vec,
            pl.BlockSpec((1, hw), lambda b, h, s, qt, kt: (0, 0)),
        ],
        out_specs=pl.BlockSpec((tq, heads * hw), q_map),
        scratch_shapes=[
            pltpu.VMEM((heads, 2 * tq, hw), BF16),
            pltpu.VMEM((heads, 2 * tq, hw), F32),
            pltpu.VMEM((heads, 2 * tq, hw), F32),
            pltpu.VMEM((heads, 2 * tq, hw), F32),
        ],
    )
    return pl.pallas_call(
        functools.partial(_attn_p_body, tq=tq, head_dim=head_dim, heads=heads),
        grid_spec=grid_spec,
        out_shape=jax.ShapeDtypeStruct((n, d), BF16),
        compiler_params=_cparams("parallel", "parallel", "arbitrary"),
        name="attn_p",
    )(qi_tab, ki_tab, q_bf, k_bf, v_bf, *lam_vecs, subln_g)


def _sample_attention(q_ref, kn_ref, vn_ref, bias_ref, lam, sg, k_refs, v_refs, o_ref,
                      *, n_heads, head_dim, s_new):
    hw = 2 * head_dim
    per_head = 2 * s_new
    rows = n_heads * per_head
    nt = (((1,), (1,)), ((), ()))

    q = q_ref[...]
    r8 = lax.broadcasted_iota(jnp.int32, (per_head, hw), 0)
    l8 = lax.broadcasted_iota(jnp.int32, (per_head, hw), 1)
    own_map = _div_pow2(l8, head_dim) == _div_pow2(r8, s_new)
    blocks = []
    for h in range(n_heads):
        qh = q[:, h * hw:(h + 1) * hw]
        blk = jnp.zeros((per_head, hw), F32)
        for t in range(s_new):
            blk = jnp.where(own_map & (_mod_pow2(r8, s_new) == t), qh[t:t + 1, :], blk)
        blocks.append(blk)
    qm = jnp.concatenate(blocks, axis=0).astype(BF16)

    n_new = kn_ref.shape[0]
    s_own = lax.dot_general(qm, kn_ref[...].astype(BF16), nt, preferred_element_type=F32)
    ridx = lax.broadcasted_iota(jnp.int32, (rows, n_new), 0)
    col = lax.broadcasted_iota(jnp.int32, (rows, n_new), 1)
    ok = (_mod_pow2(col, n_heads) == _div_pow2(ridx, per_head)) & (
        _div_pow2(col, n_heads) <= _mod_pow2(ridx, s_new))
    s_own = jnp.where(ok, s_own, NEG)

    bias = bias_ref[...]
    scores = [lax.dot_general(qm, k_ref[...].astype(BF16), nt, preferred_element_type=F32) + bias
              for k_ref in k_refs]
    m = jnp.max(s_own, axis=-1, keepdims=True)
    for s in scores:
        m = jnp.maximum(m, jnp.max(s, axis=-1, keepdims=True))
    p = jnp.exp(s_own - m)
    l = jnp.sum(p, axis=-1, keepdims=True)
    acc = jnp.dot(p.astype(BF16), vn_ref[...].astype(BF16), preferred_element_type=F32)
    for s, v_ref in zip(scores, v_refs):
        p = jnp.exp(s - m)
        l = l + jnp.sum(p, axis=-1, keepdims=True)
        acc = acc + jnp.dot(p.astype(BF16), v_ref[...].astype(BF16), preferred_element_type=F32)

    accn = acc / l
    for h in range(n_heads):
        blk = accn[h * per_head:(h + 1) * per_head]
        oh = blk[0:s_new] - lam * blk[s_new:per_head]
        o_ref[:, h * hw:(h + 1) * hw] = _rms(oh, sg) * (1.0 - LAMBDA_INIT)


def _qkv_attn_s_body(pt_ref, x_ref, g_ref, w_ref, qn_ref, kn_ref, grp_ref,
                     qs_ref, ksn_ref, vsn_ref, bias_ref, lq1, lk1, lq2, lk2, sg_ref, *rest,
                     n_pages, d_model, n_heads, head_dim, s_new, q_scale):
    del pt_ref
    k_refs = rest[:n_pages]
    v_refs = rest[n_pages:2 * n_pages]
    q_out, k_out, v_out, kb_out, vb_out, o_ref = rest[2 * n_pages:]
    _qkv_body(x_ref, g_ref, w_ref, qn_ref, kn_ref, grp_ref, q_out, k_out, v_out, kb_out, vb_out,
              d_model=d_model, head_dim=head_dim, q_scale=q_scale)
    _sample_attention(qs_ref, ksn_ref, vsn_ref, bias_ref, _lambda(lq1, lk1, lq2, lk2), sg_ref[...],
                      k_refs, v_refs, o_ref, n_heads=n_heads, head_dim=head_dim, s_new=s_new)


def _qkv_attn_sample(x, g, w_bf, qn_t, kn_t, grp, q_scale, q_s, k_new, v_new, cache_k, cache_v,
                     page_table, lam_vecs, subln_g, n_heads, head_dim):
    n, d = x.shape
    bd, s_new, _ = q_s.shape
    n_phys, page, _, hw = cache_k.shape
    prow = page * n_heads
    ck = cache_k.reshape(n_phys, prow, hw)
    cv = cache_v.reshape(n_phys, prow, hw)
    n_pages = page_table.shape[1]
    assert n % bd == 0 and (n // bd) % SUBLANES == 0, (n, bd)
    tm = n // bd
    rows = n_heads * 2 * s_new
    ridx = jnp.arange(rows, dtype=jnp.int32)[:, None] // (2 * s_new)
    col = jnp.arange(prow, dtype=jnp.int32)[None, :] % n_heads
    bias = jnp.where(col == ridx, 0.0, NEG).astype(F32)

    row = pl.BlockSpec((tm, d), lambda b, pt: (b, 0))
    const = lambda shape: pl.BlockSpec(shape, lambda b, pt: (0,) * len(shape),
                                       pipeline_mode=pl.Buffered(1))
    seq_spec = pl.BlockSpec((None, s_new, d), lambda b, pt: (b, 0, 0))
    new_spec = pl.BlockSpec((None, s_new * n_heads, hw), lambda b, pt: (b, 0, 0))
    vec = const((1, head_dim))

    def page_spec(i):
        return pl.BlockSpec((None, prow, hw), lambda b, pt: (pt[b, i], 0, 0))

    grid_spec = pltpu.PrefetchScalarGridSpec(
        num_scalar_prefetch=1,
        grid=(bd,),
        in_specs=[row, const((1, d)), const((d, 3 * d)), const((1, d)), const((1, d)), const((d, d)),
                  seq_spec, new_spec, new_spec, const((rows, prow)), vec, vec, vec, vec,
                  const((1, hw))]
                 + [page_spec(i) for i in range(n_pages)] + [page_spec(i) for i in range(n_pages)],
        out_specs=[row] * 5 + [seq_spec],
    )
    return pl.pallas_call(
        functools.partial(_qkv_attn_s_body, n_pages=n_pages, d_model=d, n_heads=n_heads,
                          head_dim=head_dim, s_new=s_new, q_scale=q_scale),
        grid_spec=grid_spec,
        out_shape=[
            jax.ShapeDtypeStruct((n, d), BF16),
            jax.ShapeDtypeStruct((n, d), F32),
            jax.ShapeDtypeStruct((n, d), F32),
            jax.ShapeDtypeStruct((n, d), BF16),
            jax.ShapeDtypeStruct((n, d), BF16),
            jax.ShapeDtypeStruct((bd, s_new, d), F32),
        ],
        compiler_params=_cparams("parallel"),
        name="qkv_attn_s",
    )(page_table, x, g, w_bf, qn_t, kn_t, grp,
      q_s, k_new.reshape(bd, s_new * n_heads, hw), v_new.reshape(bd, s_new * n_heads, hw), bias,
      *lam_vecs, subln_g, *([ck] * n_pages), *([cv] * n_pages))


def _ffn0_body(x_ref, o_ref, wo_ref, g_ref, wg_ref, wu_ref, wd_ref, y_ref, *, ff_chunks):
    x1 = x_ref[...] + jnp.dot(o_ref[...].astype(BF16), wo_ref[...], preferred_element_type=F32)
    h = _rms(x1, g_ref[...]).astype(BF16)
    y = x1
    for lo, hi in ff_chunks:
        gate = jnp.dot(h, wg_ref[:, lo:hi], preferred_element_type=F32)
        up = jnp.dot(h, wu_ref[:, lo:hi], preferred_element_type=F32)
        a = (_silu(gate) * up).astype(BF16)
        y = y + jnp.dot(a, wd_ref[lo:hi, :], preferred_element_type=F32)
    y_ref[...] = y


def _ffn0(x, o, wo_bf, g, wg_bf, wu_bf, wd_bf):
    n, d = x.shape
    f = wg_bf.shape[1]
    tm = _tile(ROW_TILE_FFN, n)
    half = (f // 256) * 128
    ff_chunks = ((0, half), (half, f))
    row = lambda i: (i, 0)
    return pl.pallas_call(
        functools.partial(_ffn0_body, ff_chunks=ff_chunks),
        grid=(n // tm,),
        in_specs=[
            pl.BlockSpec((tm, d), row),
            pl.BlockSpec((tm, d), row),
            _const_spec((d, d)),
            _const_spec((1, d)),
            _const_spec((d, f)),
            _const_spec((d, f)),
            _const_spec((f, d)),
        ],
        out_specs=pl.BlockSpec((tm, d), row),
        out_shape=jax.ShapeDtypeStruct((n, d), F32),
        compiler_params=_cparams("parallel"),
        name="ffn0",
    )(x, o, wo_bf, g, wg_bf, wu_bf, wd_bf)


def _glu_pointwise(x, g_ref, w1_ref, b1_ref, d):
    h = _rms(x, g_ref[...]).astype(BF16)
    u = jnp.dot(h, w1_ref[...], preferred_element_type=F32) + b1_ref[...]
    return u[:, :d] * jax.nn.sigmoid(u[:, d:])


def _ln_swish_pointwise(c, lng_ref, lnb_ref, w2_ref, b2_ref):
    mu = jnp.mean(c, axis=-1, keepdims=True)
    xc = c - mu
    yln = xc * lax.rsqrt(jnp.mean(xc * xc, axis=-1, keepdims=True) + EPS) * lng_ref[...] + lnb_ref[...]
    return jnp.dot(_silu(yln).astype(BF16), w2_ref[...], preferred_element_type=F32) + b2_ref[...]


def _conv_p_body(x_ref, g_ref, w1_ref, b1_ref, wdw_ref, bdw_ref, lng_ref, lnb_ref, w2_ref, b2_ref,
                 y_ref, st_ref, u_sc, sh_sc, c_sc, *, tt, width):
    t = pl.program_id(1)
    d = x_ref.shape[-1]
    keep = width - 1

    @pl.when(t == 0)
    def _():
        u_sc[0:CONV_HALO, :] = jnp.zeros((CONV_HALO, d), F32)

    x = x_ref[...]
    u_sc[CONV_HALO:CONV_HALO + tt, :] = _glu_pointwise(x, g_ref, w1_ref, b1_ref, d)

    base = CONV_HALO - keep
    span = sh_sc.shape[1]
    for b in range(1, SUBLANES):
        sh_sc[b - 1, :, :] = u_sc[b:b + span, :]

    def chunk(r, carry):
        r0 = pl.multiple_of(r * CONV_ROWS, CONV_ROWS)
        for c0 in range(0, d, LANES):
            lanes = slice(c0, c0 + LANES)
            acc = jnp.zeros((CONV_ROWS, LANES), F32)
            for j in range(width):
                a, b = divmod(base + j, SUBLANES)
                start = pl.multiple_of(r0 + SUBLANES * a, SUBLANES)
                if b == 0:
                    tap = u_sc[pl.ds(start, CONV_ROWS), lanes]
                else:
                    tap = sh_sc[b - 1, pl.ds(start, CONV_ROWS), lanes]
                acc = acc + wdw_ref[j:j + 1, lanes] * tap
            c_sc[pl.ds(r0, CONV_ROWS), lanes] = acc
        return carry

    lax.fori_loop(0, tt // CONV_ROWS, chunk, 0)

    c = c_sc[...] + bdw_ref[...]
    y_ref[...] = x + _ln_swish_pointwise(c, lng_ref, lnb_ref, w2_ref, b2_ref)
    st_ref[...] = u_sc[CONV_HALO + tt - keep:CONV_HALO + tt, :]
    u_sc[0:CONV_HALO, :] = u_sc[tt:tt + CONV_HALO, :]


def _conv_prompt(x, batch, seq, g, w1_bf, b1, wdw, bdw, lng, lnb, w2_bf, b2):
    n, d = x.shape
    width = wdw.shape[0]
    tt = _tile(CONV_TILE, seq)
    nt = seq // tt
    row = lambda b, t: (b * nt + t, 0)
    y, st = pl.pallas_call(
        functools.partial(_conv_p_body, tt=tt, width=width),
        grid=(batch, nt),
        in_specs=[
            pl.BlockSpec((tt, d), row),
            _const_spec((1, d)),
            _const_spec((d, 2 * d)),
            _const_spec((1, 2 * d)),
            _const_spec((width, d)),
            _const_spec((1, d)),
            _const_spec((1, d)),
            _const_spec((1, d)),
            _const_spec((d, d)),
            _const_spec((1, d)),
        ],
        out_specs=[
            pl.BlockSpec((tt, d), row),
            pl.BlockSpec((None, width - 1, d), lambda b, t: (b, 0, 0)),
        ],
        out_shape=[
            jax.ShapeDtypeStruct((n, d), F32),
            jax.ShapeDtypeStruct((batch, width - 1, d), F32),
        ],
        scratch_shapes=[
            pltpu.VMEM((CONV_HALO + tt, d), F32),
            pltpu.VMEM((SUBLANES - 1, CONV_HALO + tt - SUBLANES, d), F32),
            pltpu.VMEM((tt, d), F32),
        ],
        compiler_params=_cparams("parallel", "arbitrary"),
        name="conv_p",
    )(x, g, w1_bf, b1, wdw, bdw, lng, lnb, w2_bf, b2)
    return y, st


def _conv_s_body(x_ref, st_in_ref, g_ref, w1_ref, b1_ref, wdw_ref, bdw_ref, lng_ref, lnb_ref,
                 w2_ref, b2_ref, y_ref, st_out_ref, ext_sc, c_sc, *, s_new, width, n_seq):
    d = x_ref.shape[-1]
    keep = width - 1
    x = x_ref[...]
    ug = _glu_pointwise(x, g_ref, w1_ref, b1_ref, d)
    ext_sc[:, 0:keep, :] = st_in_ref[...]
    for s in range(n_seq):
        ext_sc[s, keep:keep + s_new, :] = ug[s * s_new:(s + 1) * s_new, :]
    acc = jnp.zeros((n_seq, s_new, d), F32)
    for j in range(width):
        acc = acc + wdw_ref[j:j + 1, :][None] * ext_sc[:, j:j + s_new, :]
    for s in range(n_seq):
        c_sc[s * s_new:(s + 1) * s_new, :] = acc[s]
    st_out_ref[...] = ext_sc[:, s_new:s_new + keep, :]
    c = c_sc[...] + bdw_ref[...]
    y_ref[...] = x + _ln_swish_pointwise(c, lng_ref, lnb_ref, w2_ref, b2_ref)


def _conv_sample(x, state, s_new, g, w1_bf, b1, wdw, bdw, lng, lnb, w2_bf, b2):
    n, d = x.shape
    bd = state.shape[0]
    width = wdw.shape[0]
    keep = width - 1
    ns = _tile(CONV_SEQ_BLOCK, bd)
    y, st = pl.pallas_call(
        functools.partial(_conv_s_body, s_new=s_new, width=width, n_seq=ns),
        grid=(bd // ns,),
        in_specs=[
            pl.BlockSpec((ns * s_new, d), lambda i: (i, 0)),
            pl.BlockSpec((ns, keep, d), lambda i: (i, 0, 0)),
            _const_spec((1, d)),
            _const_spec((d, 2 * d)),
            _const_spec((1, 2 * d)),
            _const_spec((width, d)),
            _const_spec((1, d)),
            _const_spec((1, d)),
            _const_spec((1, d)),
            _const_spec((d, d)),
            _const_spec((1, d)),
        ],
        out_specs=[
            pl.BlockSpec((ns * s_new, d), lambda i: (i, 0)),
            pl.BlockSpec((ns, keep, d), lambda i: (i, 0, 0)),
        ],
        out_shape=[
            jax.ShapeDtypeStruct((n, d), F32),
            jax.ShapeDtypeStruct((bd, keep, d), F32),
        ],
        scratch_shapes=[
            pltpu.VMEM((ns, keep + s_new, d), F32),
            pltpu.VMEM((ns * s_new, d), F32),
        ],
        compiler_params=_cparams("parallel"),
        name="conv_s",
    )(x, state, g, w1_bf, b1, wdw, bdw, lng, lnb, w2_bf, b2)
    return y, st


def _row_specs(tm, d, n_first_tiles):
    first = pl.BlockSpec((tm, d), lambda i, *_: (jnp.minimum(i, n_first_tiles - 1), 0))
    second = pl.BlockSpec((tm, d), lambda i, *_: (jnp.maximum(i - n_first_tiles, 0), 0))
    return first, second


def _rows(i, n_first_tiles, first_ref, second_ref):
    return jnp.where(i < n_first_tiles, first_ref[...], second_ref[...])


def _router_body(xa_ref, xb_ref, g_ref, wrt_ref, tri_ref, idx_ref, gate_ref, rank_ref, cnt_ref, cnt_sc,
                 *, n_first_tiles):
    i = pl.program_id(0)

    @pl.when(i == 0)
    def _():
        cnt_sc[...] = jnp.zeros(cnt_sc.shape, F32)

    h = _rms(_rows(i, n_first_tiles, xa_ref, xb_ref), g_ref[...])
    logits = lax.dot_general(wrt_ref[...], h, (((1,), (1,)), ((), ())),
                             precision=lax.Precision.HIGHEST,
                             preferred_element_type=F32)
    n_e = float(logits.shape[0])
    eidx = lax.broadcasted_iota(jnp.int32, logits.shape, 0).astype(F32)
    v1 = jnp.max(logits, axis=0, keepdims=True)
    i1 = jnp.min(jnp.where(logits == v1, eidx, n_e), axis=0, keepdims=True)
    rest = jnp.where(eidx == i1, -jnp.inf, logits)
    v2 = jnp.max(rest, axis=0, keepdims=True)
    i2 = jnp.min(jnp.where(rest == v2, eidx, n_e), axis=0, keepdims=True)
    e = jnp.exp(v2 - v1)
    g1 = 1.0 / (1.0 + e)
    idx_ref[0:1, :] = i1.astype(jnp.int32)
    idx_ref[1:2, :] = i2.astype(jnp.int32)
    gate_ref[0:1, :] = g1
    gate_ref[1:2, :] = e * g1

    hot1 = (eidx == i1).astype(F32)
    hot2 = (eidx == i2).astype(F32)
    tri = tri_ref[...]
    incl1 = jnp.dot(hot1.astype(BF16), tri, preferred_element_type=F32)
    incl2 = jnp.dot(hot2.astype(BF16), tri, preferred_element_type=F32)
    cnt = cnt_sc[...]
    c1 = jnp.sum(hot1, axis=1, keepdims=True)
    c2 = jnp.sum(hot2, axis=1, keepdims=True)
    r1 = jnp.sum(hot1 * (cnt + incl1), axis=0, keepdims=True) - 1.0
    r2 = jnp.sum(hot2 * (cnt + c1 + incl2), axis=0, keepdims=True) - 1.0
    rank_ref[0:1, :] = r1.astype(jnp.int32)
    rank_ref[1:2, :] = r2.astype(jnp.int32)
    cnt_new = cnt + c1 + c2
    cnt_sc[...] = cnt_new
    cnt_ref[...] = cnt_new.astype(jnp.int32)


def _router(xa, xb, g, w_router_t):
    d = xa.shape[1]
    n = xa.shape[0] + xb.shape[0]
    n_e = w_router_t.shape[0]
    tm = _tile(ROUTER_TILE, xb.shape[0])
    n_first = xa.shape[0] // _tile(tm, xa.shape[0])
    tri = (jnp.arange(tm)[:, None] <= jnp.arange(tm)[None, :]).astype(BF16)
    tok = pl.BlockSpec((TOP_K, tm), lambda i: (0, i))
    return pl.pallas_call(
        functools.partial(_router_body, n_first_tiles=n_first),
        grid=(n // tm,),
        in_specs=[
            *_row_specs(tm, d, n_first),
            _const_spec((1, d)),
            _const_spec((n_e, d)),
            _const_spec((tm, tm)),
        ],
        out_specs=[tok, tok, tok, pl.BlockSpec((n_e, 1), lambda i: (0, 0))],
        out_shape=[
            jax.ShapeDtypeStruct((TOP_K, n), jnp.int32),
            jax.ShapeDtypeStruct((TOP_K, n), F32),
            jax.ShapeDtypeStruct((TOP_K, n), jnp.int32),
            jax.ShapeDtypeStruct((n_e, 1), jnp.int32),
        ],
        scratch_shapes=[pltpu.VMEM((n_e, 1), F32)],
        compiler_params=_cparams("arbitrary"),
        name="router",
    )(xa, xb, g, w_router_t, tri)


def _dispatch_body(dest_ref, xa_ref, xb_ref, g_ref, gate_ref, xs_in, xs_out, buf_sc, sem,
                   *, tm, d, n_first_tiles):
    del xs_in
    i = pl.program_id(0)
    slot = lax.rem(i, 2)
    h = _rms(_rows(i, n_first_tiles, xa_ref, xb_ref), g_ref[...])
    gate = gate_ref[...]
    for k in range(TOP_K):
        buf_sc[slot, k, :, 0:d] = h
        buf_sc[slot, k, :, d:] = jnp.broadcast_to(gate[:, k:k + 1], (tm, GATE_LANES))

    def issue(t, carry):
        for k in range(TOP_K):
            dst = dest_ref[0, k * tm + t]
            pltpu.make_async_copy(buf_sc.at[slot, k, pl.ds(t, 1), :],
                                  xs_out.at[pl.ds(dst, 1), :], sem.at[slot]).start()
        return carry

    lax.fori_loop(0, tm, issue, 0, unroll=8)

    def wait_rows(which):
        for k in range(TOP_K):
            pltpu.make_async_copy(buf_sc.at[which, k], xs_out.at[pl.ds(0, tm), :],
                                  sem.at[which]).wait()

    @pl.when(i > 0)
    def _():
        wait_rows(1 - slot)

    @pl.when(i == pl.num_programs(0) - 1)
    def _():
        wait_rows(slot)


def _dispatch(xa, xb, g, gate_cols, dest_tiles, p_rows, tm):
    d = xa.shape[1]
    n = xa.shape[0] + xb.shape[0]
    n_first = xa.shape[0] // _tile(tm, xa.shape[0])
    dg = d + GATE_LANES
    xs0 = jnp.zeros((p_rows, dg), F32)
    return pl.pallas_call(
        functools.partial(_dispatch_body, tm=tm, d=d, n_first_tiles=n_first),
        grid=(n // tm,),
        in_specs=[
            pl.BlockSpec((None, 1, TOP_K * tm), lambda i: (i, 0, 0), memory_space=pltpu.SMEM),
            *_row_specs(tm, d, n_first),
            _const_spec((1, d)),
            pl.BlockSpec((tm, TOP_K), lambda i: (i, 0)),
            pl.BlockSpec(memory_space=pl.ANY),
        ],
        out_specs=pl.BlockSpec(memory_space=pl.ANY),
        out_shape=jax.ShapeDtypeStruct((p_rows, dg), F32),
        scratch_shapes=[
            pltpu.VMEM((2, TOP_K, tm, dg), F32),
            pltpu.SemaphoreType.DMA((2,)),
        ],
        input_output_aliases={5: 0},
        compiler_params=_cparams("arbitrary"),
        name="dispatch",
    )(dest_tiles, xa, xb, g, gate_cols, xs0)


def _moe_body(te_ref, tv_ref, ts_ref, x_ref, wg_ref, wu_ref, wd_ref, y_ref, xb_sc, acc_sc):
    t = pl.program_id(0)
    c = pl.program_id(1)
    d = acc_sc.shape[1]

    @pl.when(tv_ref[t] == 1)
    def _():
        @pl.when(c == 0)
        def _():
            acc_sc[...] = jnp.zeros(acc_sc.shape, F32)
            xb_sc[...] = x_ref[:, 0:d].astype(BF16)

        x = xb_sc[...]
        gate = jnp.dot(x, wg_ref[...].astype(BF16), preferred_element_type=F32)
        up = jnp.dot(x, wu_ref[...].astype(BF16), preferred_element_type=F32)
        a = (_silu(gate) * up).astype(BF16)
        acc_sc[...] += jnp.dot(a, wd_ref[...].astype(BF16), preferred_element_type=F32)

        @pl.when(c == pl.num_programs(1) - 1)
        def _():
            y_ref[...] = acc_sc[...] * x_ref[:, d:d + 1]

    @pl.when((tv_ref[t] == 0) & (c == pl.num_programs(1) - 1))
    def _():
        y_ref[...] = jnp.zeros(y_ref.shape, F32)


def _moe(xs, tile_expert, tile_valid, tile_src, w_gate_e, w_up_e, w_down_e, tm):
    p_rows, dg = xs.shape
    n_e, d, f = w_gate_e.shape
    fc = _tile(MOE_FF_CHUNK, f)
    nc = f // fc
    n_tiles = p_rows // tm

    def chunk_of(t, c, tv):
        return jnp.where(tv[t] == 1, c, nc - 1)

    grid_spec = pltpu.PrefetchScalarGridSpec(
        num_scalar_prefetch=3,
        grid=(n_tiles, nc),
        in_specs=[
            pl.BlockSpec((tm, dg), lambda t, c, te, tv, ts: (ts[t], 0)),
            pl.BlockSpec((None, d, fc), lambda t, c, te, tv, ts: (te[t], 0, chunk_of(t, c, tv))),
            pl.BlockSpec((None, d, fc), lambda t, c, te, tv, ts: (te[t], 0, chunk_of(t, c, tv))),
            pl.BlockSpec((None, fc, d), lambda t, c, te, tv, ts: (te[t], chunk_of(t, c, tv), 0)),
        ],
        out_specs=pl.BlockSpec((tm, d), lambda t, c, te, tv, ts: (t, 0)),
        scratch_shapes=[pltpu.VMEM((tm, d), BF16), pltpu.VMEM((tm, d), F32)],
    )
    return pl.pallas_call(
        _moe_body,
        grid_spec=grid_spec,
        out_shape=jax.ShapeDtypeStruct((p_rows, d), F32),
        compiler_params=_cparams("arbitrary", "arbitrary"),
        name="moe",
    )(tile_expert, tile_valid, tile_src, xs, w_gate_e, w_up_e, w_down_e)


def _combine_body(dest_ref, dest_next_ref, xa_ref, xb_ref, ys_hbm, oa_ref, ob_ref, buf_sc, sem,
                  *, tm, n_first_tiles):
    i = pl.program_id(0)
    n = pl.num_programs(0)
    slot = lax.rem(i, 2)

    def start_gather(dref, which):
        def issue(t, carry):
            for k in range(TOP_K):
                src = dref[0, k * tm + t]
                pltpu.make_async_copy(ys_hbm.at[pl.ds(src, 1), :],
                                      buf_sc.at[which, k, pl.ds(t, 1), :], sem.at[which]).start()
            return carry

        lax.fori_loop(0, tm, issue, 0, unroll=8)

    @pl.when(i == 0)
    def _():
        start_gather(dest_ref, slot)

    @pl.when(i + 1 < n)
    def _():
        start_gather(dest_next_ref, 1 - slot)

    for k in range(TOP_K):
        pltpu.make_async_copy(ys_hbm.at[pl.ds(0, tm), :], buf_sc.at[slot, k], sem.at[slot]).wait()
    y = _rows(i, n_first_tiles, xa_ref, xb_ref) + (buf_sc[slot, 0] + buf_sc[slot, 1])

    @pl.when(i < n_first_tiles)
    def _():
        oa_ref[...] = y

    @pl.when(i >= n_first_tiles)
    def _():
        ob_ref[...] = y


def _combine(xa, xb, ys, dest_tiles, tm):
    d = xa.shape[1]
    n_tiles = (xa.shape[0] + xb.shape[0]) // tm
    n_first = xa.shape[0] // _tile(tm, xa.shape[0])
    dest_spec = lambda f: pl.BlockSpec((None, 1, TOP_K * tm), lambda i: (f(i), 0, 0),
                                       memory_space=pltpu.SMEM)
    rows_a, rows_b = _row_specs(tm, d, n_first)
    return pl.pallas_call(
        functools.partial(_combine_body, tm=tm, n_first_tiles=n_first),
        grid=(n_tiles,),
        in_specs=[
            dest_spec(lambda i: i),
            dest_spec(lambda i: jnp.minimum(i + 1, n_tiles - 1)),
            rows_a, rows_b,
            pl.BlockSpec(memory_space=pl.ANY),
        ],
        out_specs=list(_row_specs(tm, d, n_first)),
        out_shape=[jax.ShapeDtypeStruct(xa.shape, F32), jax.ShapeDtypeStruct(xb.shape, F32)],
        scratch_shapes=[
            pltpu.VMEM((2, TOP_K, tm, d), F32),
            pltpu.SemaphoreType.DMA((2,)),
        ],
        compiler_params=_cparams("arbitrary"),
        name="combine",
    )(dest_tiles, dest_tiles, xa, xb, ys)


def _moe_layer(xa, xb, g, w_router, w_gate_e, w_up_e, w_down_e):
    d = xa.shape[1]
    n = xa.shape[0] + xb.shape[0]
    n_e = w_router.shape[1]
    tm = min(MOE_TILE, n)
    idx, gates, rank, counts = _router(xa, xb, g, w_router.T)

    counts = counts[:, 0]
    padded = ((counts + tm - 1) // tm) * tm
    ends = jnp.cumsum(padded)
    offs = ends - padded
    off_of = jnp.zeros_like(idx)
    for e in range(n_e):
        off_of = jnp.where(idx == e, offs[e], off_of)
    dest = off_of + rank
    n_tiles = (TOP_K * n) // tm + n_e
    p_rows = n_tiles * tm
    tile_start = jnp.arange(n_tiles, dtype=jnp.int32) * tm
    tile_valid = (tile_start < ends[-1]).astype(jnp.int32)
    last_valid = jnp.maximum(jnp.sum(tile_valid) - 1, 0)
    tile_src = jnp.minimum(jnp.arange(n_tiles, dtype=jnp.int32), last_valid)
    tile_expert = jnp.minimum(
        jnp.sum((tile_src * tm)[:, None] >= ends[None, :], axis=1), n_e - 1).astype(jnp.int32)

    td = _tile(DISPATCH_TILE, xb.shape[0])
    dest_tiles = dest.reshape(TOP_K, n // td, td).transpose(1, 0, 2).reshape(n // td, 1, TOP_K * td)
    xs = _dispatch(xa, xb, g, gates.T, dest_tiles, p_rows, td)
    ys = _moe(xs, tile_expert, tile_valid, tile_src, w_gate_e, w_up_e, w_down_e, tm)
    return _combine(xa, xb, ys, dest_tiles, td)


def kernel(x_prompt, x_sample, cache_k, cache_v, state_conv, page_table, norm_attn, w_qkv, q_norm, k_norm, lambda_q1, lambda_k1, lambda_q2, lambda_k2, subln_g, w_o, norm_ffn0, w_gate0, w_up0, w_down0, norm_conv, w_pw1, b_pw1, w_dw, b_dw, ln_g, ln_b, w_pw2, b_pw2, norm_ffn1, w_router, w_gate_e, w_up_e, w_down_e):
    batch, seq, d = x_prompt.shape
    bd, s_new, _ = x_sample.shape
    n_heads = cache_k.shape[2]
    head_dim = q_norm.shape[0]
    hw = 2 * head_dim
    n_p = batch * seq
    n_s = bd * s_new

    row = lambda v: v.reshape(1, -1)
    bf = lambda w: w.astype(BF16)
    xp = x_prompt.reshape(n_p, d)
    xs = x_sample.reshape(n_s, d)
    grp = jnp.kron(jnp.eye(d // head_dim, dtype=F32), jnp.ones((head_dim, head_dim), F32)).astype(BF16)
    qn_t = row(jnp.tile(q_norm, d // head_dim))
    kn_t = row(jnp.tile(k_norm, d // head_dim))
    lam_vecs = (row(lambda_q1), row(lambda_k1), row(lambda_q2), row(lambda_k2))
    sg = row(subln_g)
    w_qkv_bf = bf(w_qkv)

    qk_scale = head_dim ** -0.5
    qs, ks, vs, _, _ = _qkv(xs, row(norm_attn), w_qkv_bf, qn_t, kn_t, grp, head_dim, F32, qk_scale)
    qp, kp, vp, kpb, vpb, osm = _qkv_attn_sample(
        xp, row(norm_attn), w_qkv_bf, qn_t, kn_t, grp, qk_scale * math.log2(math.e),
        qs.reshape(bd, s_new, d), ks, vs, cache_k, cache_v, page_table, lam_vecs, sg, n_heads, head_dim)
    op = _attn_prompt(qp, kpb, vpb, lam_vecs, sg, batch, seq, n_heads, head_dim)

    ffn0_w = (bf(w_o), row(norm_ffn0), bf(w_gate0), bf(w_up0), bf(w_down0))
    xp = _ffn0(xp, op, *ffn0_w)
    xs = _ffn0(xs, osm.reshape(n_s, d), *ffn0_w)

    conv_w = (row(norm_conv), bf(w_pw1), row(b_pw1), w_dw, row(b_dw), row(ln_g), row(ln_b),
              bf(w_pw2), row(b_pw2))
    xp, conv_prompt = _conv_prompt(xp, batch, seq, *conv_w)
    xs, conv_sample = _conv_sample(xs, state_conv, s_new, *conv_w)

    xp, xs = _moe_layer(xp, xs, row(norm_ffn1), w_router, w_gate_e, w_up_e, w_down_e)

    return (xp.reshape(batch, seq, d), xs.reshape(bd, s_new, d),
            kp.reshape(batch, seq, n_heads, hw), vp.reshape(batch, seq, n_heads, hw),
            ks.reshape(bd, s_new, n_heads, hw), vs.reshape(bd, s_new, n_heads, hw),
            conv_prompt, conv_sample)
```

```python
import functools
import math

import jax
import jax.numpy as jnp
from jax import lax
from jax.experimental import pallas as pl
from jax.experimental.pallas import tpu as pltpu

F32 = jnp.float32
BF16 = jnp.bfloat16

EPS = 1e-6
LAMBDA_INIT = 0.8 - 0.6 * math.exp(-0.3 * 0)
TOP_K = 2
NEG = -1e30
SUBLANES = 8
LANES = 128

VMEM_LIMIT_BYTES = 56 * 1024 * 1024

ROW_TILE_QKV = 256
ROW_TILE_FFN = 512
ATTN_TILE = 512
ATTN_HEADS_PER_STEP = 8
ATTN_ROW_CHUNK = 512
CONV_TILE = 512
CONV_HALO = 32
CONV_ROWS = 16
CONV_SEQ_BLOCK = 32
ROUTER_TILE = 512
MOE_TILE = 1024
MOE_FF_CHUNK = 512
DISPATCH_TILE = 512
ROW_DMA_UNROLL = 16
GATE_LANES = 128


def _cparams(*sem):
    return pltpu.CompilerParams(dimension_semantics=sem, vmem_limit_bytes=VMEM_LIMIT_BYTES)


def _const_spec(shape):
    nd = len(shape)
    return pl.BlockSpec(shape, lambda *_: (0,) * nd, pipeline_mode=pl.Buffered(1))


def _rms(x, g):
    return x * lax.rsqrt(jnp.mean(x * x, axis=-1, keepdims=True) + EPS) * g


def _silu(x):
    return x * jax.nn.sigmoid(x)


def _tile(limit, n):
    t = min(limit, n)
    assert n % t == 0, (n, t)
    return t


def _div_pow2(x, c):
    assert c > 0 and c & (c - 1) == 0, c
    return lax.shift_right_logical(x, c.bit_length() - 1)


def _mod_pow2(x, c):
    assert c > 0 and c & (c - 1) == 0, c
    return lax.bitwise_and(x, c - 1)


def _qkv_body(x_ref, g_ref, w_ref, qn_ref, kn_ref, grp_ref, q_out, k_out, v_out, kb_out, vb_out,
              *, d_model, head_dim, q_scale):
    x = x_ref[...]
    h = _rms(x, g_ref[...]).astype(BF16)
    qkv = jnp.dot(h, w_ref[...], preferred_element_type=F32)
    q = qkv[:, :d_model]
    k = qkv[:, d_model:2 * d_model]
    v = qkv[:, 2 * d_model:]
    grp = grp_ref[...]

    def head_norm(t, gain):
        ssq = jnp.dot((t * t).astype(BF16), grp, preferred_element_type=F32)
        return t * lax.rsqrt(ssq * (1.0 / head_dim) + EPS) * gain

    qn = head_norm(q, qn_ref[...]) * q_scale
    kn = head_norm(k, kn_ref[...])
    q_out[...] = qn.astype(q_out.dtype)
    k_out[...] = kn
    v_out[...] = v
    kb_out[...] = kn.astype(BF16)
    vb_out[...] = v.astype(BF16)


def _qkv(x, g, w_bf, qn_t, kn_t, grp, head_dim, q_dtype, q_scale):
    n, d = x.shape
    tm = _tile(ROW_TILE_QKV, n)
    row = lambda i: (i, 0)
    out = pl.pallas_call(
        functools.partial(_qkv_body, d_model=d, head_dim=head_dim, q_scale=q_scale),
        grid=(n // tm,),
        in_specs=[
            pl.BlockSpec((tm, d), row),
            _const_spec((1, d)),
            _const_spec((d, 3 * d)),
            _const_spec((1, d)),
            _const_spec((1, d)),
            _const_spec((d, d)),
        ],
        out_specs=[pl.BlockSpec((tm, d), row)] * 5,
        out_shape=[
            jax.ShapeDtypeStruct((n, d), q_dtype),
            jax.ShapeDtypeStruct((n, d), F32),
            jax.ShapeDtypeStruct((n, d), F32),
            jax.ShapeDtypeStruct((n, d), BF16),
            jax.ShapeDtypeStruct((n, d), BF16),
        ],
        compiler_params=_cparams("parallel"),
        name="qkv",
    )(x, g, w_bf, qn_t, kn_t, grp)
    return out


def _lambda(lq1, lk1, lq2, lk2):
    a = jnp.sum(lq1[...] * lk1[...], axis=-1, keepdims=True)
    b = jnp.sum(lq2[...] * lk2[...], axis=-1, keepdims=True)
    return jnp.exp(a) - jnp.exp(b) + LAMBDA_INIT


def _attn_p_body(qi_tab, ki_tab, q_ref, k_ref, v_ref, lq1, lk1, lq2, lk2, sg_ref, o_ref,
                 qq_sc, m_sc, l_sc, acc_sc, *, tq, head_dim, heads):
    step = pl.program_id(2)
    qi = qi_tab[step]
    ki = ki_tab[step]
    hw = 2 * head_dim
    rc = _tile(ATTN_ROW_CHUNK, tq)

    @pl.when(ki == 0)
    def _():
        for h in range(heads):
            q = q_ref[:, h * hw:(h + 1) * hw]
            lane = lax.broadcasted_iota(jnp.int32, q.shape, 1)
            zero = jnp.zeros_like(q)
            qq_sc[h, 0:tq, :] = jnp.where(lane < head_dim, q, zero)
            qq_sc[h, tq:2 * tq, :] = jnp.where(lane >= head_dim, q, zero)
        m_sc[...] = jnp.full(m_sc.shape, NEG, F32)
        l_sc[...] = jnp.zeros(l_sc.shape, F32)
        acc_sc[...] = jnp.zeros(acc_sc.shape, F32)

    def update(diagonal):
        for h in range(heads):
            lanes = slice(h * hw, (h + 1) * hw)
            for r0 in range(0, 2 * tq, rc):
                q0 = r0 % tq
                nk = q0 + rc if diagonal else tq
                rows = slice(r0, r0 + rc)
                s = lax.dot_general(qq_sc[h, rows, :], k_ref[0:nk, lanes], (((1,), (1,)), ((), ())),
                                    preferred_element_type=F32)
                if diagonal:
                    row = lax.broadcasted_iota(jnp.int32, s.shape, 0) + q0
                    col = lax.broadcasted_iota(jnp.int32, s.shape, 1)
                    s = jnp.where(col <= row, s, NEG)
                m_prev = m_sc[h, rows, :]
                m_new = jnp.maximum(m_prev, jnp.max(s, axis=-1, keepdims=True))
                alpha = jnp.exp2(m_prev - m_new)
                p = jnp.exp2(s - jnp.concatenate([m_new] * (nk // hw), axis=1)).astype(BF16)
                v1 = jnp.concatenate([v_ref[0:nk, lanes], jnp.ones((nk, hw), BF16)], axis=1)
                pv = jnp.dot(p, v1, preferred_element_type=F32)
                l_sc[h, rows, :] = alpha * l_sc[h, rows, :] + pv[:, hw:]
                acc_sc[h, rows, :] = alpha * acc_sc[h, rows, :] + pv[:, :hw]
                m_sc[h, rows, :] = m_new

    @pl.when(ki < qi)
    def _():
        update(False)

    @pl.when(ki == qi)
    def _():
        update(True)
        lam = _lambda(lq1, lk1, lq2, lk2)
        for h in range(heads):
            o1 = acc_sc[h, 0:tq, :] / l_sc[h, 0:tq, :]
            o2 = acc_sc[h, tq:2 * tq, :] / l_sc[h, tq:2 * tq, :]
            o = o1 - lam * o2
            o_ref[:, h * hw:(h + 1) * hw] = (_rms(o, sg_ref[...]) * (1.0 - LAMBDA_INIT)).astype(o_ref.dtype)


def _attn_prompt(q_bf, k_bf, v_bf, lam_vecs, subln_g, batch, seq, n_heads, head_dim):
    n, d = q_bf.shape
    hw = 2 * head_dim
    tq = _tile(ATTN_TILE, seq)
    nq = seq // tq
    heads = _tile(ATTN_HEADS_PER_STEP, n_heads)
    pairs = [(i, j) for i in range(nq) for j in range(i + 1)]
    qi_tab = jnp.asarray([p[0] for p in pairs], jnp.int32)
    ki_tab = jnp.asarray([p[1] for p in pairs], jnp.int32)
    q_map = lambda b, h, s, qt, kt: (b * nq + qt[s], h)
    kv_map = lambda b, h, s, qt, kt: (b * nq + kt[s], h)
    vec = pl.BlockSpec((1, head_dim), lambda b, h, s, qt, kt: (0, 0))
    grid_spec = pltpu.PrefetchScalarGridSpec(
        num_scalar_prefetch=2,
        grid=(batch, n_heads // heads, len(pairs)),
        in_specs=[
            pl.BlockSpec((tq, heads * hw), q_map),
            pl.BlockSpec((tq, heads * hw), kv_map),
            pl.BlockSpec((tq, heads * hw), kv_map),
            vec, vec, vec, vec,
            pl.BlockSpec((1, hw), lambda b, h, s, qt, kt: (0, 0)),
        ],
        out_specs=pl.BlockSpec((tq, heads * hw), q_map),
        scratch_shapes=[
            pltpu.VMEM((heads, 2 * tq, hw), BF16),
            pltpu.VMEM((heads, 2 * tq, hw), F32),
            pltpu.VMEM((heads, 2 * tq, hw), F32),
            pltpu.VMEM((heads, 2 * tq, hw), F32),
        ],
    )
    return pl.pallas_call(
        functools.partial(_attn_p_body, tq=tq, head_dim=head_dim, heads=heads),
        grid_spec=grid_spec,
        out_shape=jax.ShapeDtypeStruct((n, d), BF16),
        compiler_params=_cparams("parallel", "parallel", "arbitrary"),
        name="attn_p",
    )(qi_tab, ki_tab, q_bf, k_bf, v_bf, *lam_vecs, subln_g)


def _sample_attention(q_ref, kn_ref, vn_ref, bias_ref, lam, sg, k_refs, v_refs, o_ref,
                      *, n_heads, head_dim, s_new):
    hw = 2 * head_dim
    per_head = 2 * s_new
    rows = n_heads * per_head
    nt = (((1,), (1,)), ((), ()))

    q = q_ref[...]
    r8 = lax.broadcasted_iota(jnp.int32, (per_head, hw), 0)
    l8 = lax.broadcasted_iota(jnp.int32, (per_head, hw), 1)
    own_map = _div_pow2(l8, head_dim) == _div_pow2(r8, s_new)
    blocks = []
    for h in range(n_heads):
        qh = q[:, h * hw:(h + 1) * hw]
        blk = jnp.zeros((per_head, hw), F32)
        for t in range(s_new):
            blk = jnp.where(own_map & (_mod_pow2(r8, s_new) == t), qh[t:t + 1, :], blk)
        blocks.append(blk)
    qm = jnp.concatenate(blocks, axis=0).astype(BF16)

    n_new = kn_ref.shape[0]
    s_own = lax.dot_general(qm, kn_ref[...].astype(BF16), nt, preferred_element_type=F32)
    ridx = lax.broadcasted_iota(jnp.int32, (rows, n_new), 0)
    col = lax.broadcasted_iota(jnp.int32, (rows, n_new), 1)
    ok = (_mod_pow2(col, n_heads) == _div_pow2(ridx, per_head)) & (
        _div_pow2(col, n_heads) <= _mod_pow2(ridx, s_new))
    s_own = jnp.where(ok, s_own, NEG)

    bias = bias_ref[...]
    scores = [lax.dot_general(qm, k_ref[...].astype(BF16), nt, preferred_element_type=F32) + bias
              for k_ref in k_refs]
    m = jnp.max(s_own, axis=-1, keepdims=True)
    for s in scores:
        m = jnp.maximum(m, jnp.max(s, axis=-1, keepdims=True))
    p = jnp.exp(s_own - m)
    l = jnp.sum(p, axis=-1, keepdims=True)
    acc = jnp.dot(p.astype(BF16), vn_ref[...].astype(BF16), preferred_element_type=F32)
    for s, v_ref in zip(scores, v_refs):
        p = jnp.exp(s - m)
        l = l + jnp.sum(p, axis=-1, keepdims=True)
        acc = acc + jnp.dot(p.astype(BF16), v_ref[...].astype(BF16), preferred_element_type=F32)

    accn = acc / l
    for h in range(n_heads):
        blk = accn[h * per_head:(h + 1) * per_head]
        oh = blk[0:s_new] - lam * blk[s_new:per_head]
        o_ref[:, h * hw:(h + 1) * hw] = _rms(oh, sg) * (1.0 - LAMBDA_INIT)


def _qkv_attn_s_body(pt_ref, x_ref, g_ref, w_ref, qn_ref, kn_ref, grp_ref,
                     qs_ref, ksn_ref, vsn_ref, bias_ref, lq1, lk1, lq2, lk2, sg_ref, *rest,
                     n_pages, d_model, n_heads, head_dim, s_new, q_scale):
    del pt_ref
    k_refs = rest[:n_pages]
    v_refs = rest[n_pages:2 * n_pages]
    q_out, k_out, v_out, kb_out, vb_out, o_ref = rest[2 * n_pages:]
    _qkv_body(x_ref, g_ref, w_ref, qn_ref, kn_ref, grp_ref, q_out, k_out, v_out, kb_out, vb_out,
              d_model=d_model, head_dim=head_dim, q_scale=q_scale)
    _sample_attention(qs_ref, ksn_ref, vsn_ref, bias_ref, _lambda(lq1, lk1, lq2, lk2), sg_ref[...],
                      k_refs, v_refs, o_ref, n_heads=n_heads, head_dim=head_dim, s_new=s_new)


def _qkv_attn_sample(x, g, w_bf, qn_t, kn_t, grp, q_scale, q_s, k_new, v_new, cache_k, cache_v,
                     page_table, lam_vecs, subln_g, n_heads, head_dim):
    n, d = x.shape
    bd, s_new, _ = q_s.shape
    n_phys, page, _, hw = cache_k.shape
    prow = page * n_heads
    ck = cache_k.reshape(n_phys, prow, hw)
    cv = cache_v.reshape(n_phys, prow, hw)
    n_pages = page_table.shape[1]
    assert n % bd == 0 and (n // bd) % SUBLANES == 0, (n, bd)
    tm = n // bd
    rows = n_heads * 2 * s_new
    ridx = jnp.arange(rows, dtype=jnp.int32)[:, None] // (2 * s_new)
    col = jnp.arange(prow, dtype=jnp.int32)[None, :] % n_heads
    bias = jnp.where(col == ridx, 0.0, NEG).astype(F32)

    row = pl.BlockSpec((tm, d), lambda b, pt: (b, 0))
    const = lambda shape: pl.BlockSpec(shape, lambda b, pt: (0,) * len(shape),
                                       pipeline_mode=pl.Buffered(1))
    seq_spec = pl.BlockSpec((None, s_new, d), lambda b, pt: (b, 0, 0))
    new_spec = pl.BlockSpec((None, s_new * n_heads, hw), lambda b, pt: (b, 0, 0))
    vec = const((1, head_dim))

    def page_spec(i):
        return pl.BlockSpec((None, prow, hw), lambda b, pt: (pt[b, i], 0, 0))

    grid_spec = pltpu.PrefetchScalarGridSpec(
        num_scalar_prefetch=1,
        grid=(bd,),
        in_specs=[row, const((1, d)), const((d, 3 * d)), const((1, d)), const((1, d)), const((d, d)),
                  seq_spec, new_spec, new_spec, const((rows, prow)), vec, vec, vec, vec,
                  const((1, hw))]
                 + [page_spec(i) for i in range(n_pages)] + [page_spec(i) for i in range(n_pages)],
        out_specs=[row] * 5 + [seq_spec],
    )
    return pl.pallas_call(
        functools.partial(_qkv_attn_s_body, n_pages=n_pages, d_model=d, n_heads=n_heads,
                          head_dim=head_dim, s_new=s_new, q_scale=q_scale),
        grid_spec=grid_spec,
        out_shape=[
            jax.ShapeDtypeStruct((n, d), BF16),
            jax.ShapeDtypeStruct((n, d), F32),
            jax.ShapeDtypeStruct((n, d), F32),
            jax.ShapeDtypeStruct((n, d), BF16),
            jax.ShapeDtypeStruct((n, d), BF16),
            jax.ShapeDtypeStruct((bd, s_new, d), F32),
        ],
        compiler_params=_cparams("parallel"),
        name="qkv_attn_s",
    )(page_table, x, g, w_bf, qn_t, kn_t, grp,
      q_s, k_new.reshape(bd, s_new * n_heads, hw), v_new.reshape(bd, s_new * n_heads, hw), bias,
      *lam_vecs, subln_g, *([ck] * n_pages), *([cv] * n_pages))


def _ffn0_body(x_ref, o_ref, wo_ref, g_ref, wg_ref, wu_ref, wd_ref, y_ref, *, ff_chunks):
    x1 = x_ref[...] + jnp.dot(o_ref[...].astype(BF16), wo_ref[...], preferred_element_type=F32)
    h = _rms(x1, g_ref[...]).astype(BF16)
    y = x1
    for lo, hi in ff_chunks:
        gate = jnp.dot(h, wg_ref[:, lo:hi], preferred_element_type=F32)
        up = jnp.dot(h, wu_ref[:, lo:hi], preferred_element_type=F32)
        a = (_silu(gate) * up).astype(BF16)
        y = y + jnp.dot(a, wd_ref[lo:hi, :], preferred_element_type=F32)
    y_ref[...] = y


def _ffn0(x, o, wo_bf, g, wg_bf, wu_bf, wd_bf):
    n, d = x.shape
    f = wg_bf.shape[1]
    tm = _tile(ROW_TILE_FFN, n)
    half = (f // 256) * 128
    ff_chunks = ((0, half), (half, f))
    row = lambda i: (i, 0)
    return pl.pallas_call(
        functools.partial(_ffn0_body, ff_chunks=ff_chunks),
        grid=(n // tm,),
        in_specs=[
            pl.BlockSpec((tm, d), row),
            pl.BlockSpec((tm, d), row),
            _const_spec((d, d)),
            _const_spec((1, d)),
            _const_spec((d, f)),
            _const_spec((d, f)),
            _const_spec((f, d)),
        ],
        out_specs=pl.BlockSpec((tm, d), row),
        out_shape=jax.ShapeDtypeStruct((n, d), F32),
        compiler_params=_cparams("parallel"),
        name="ffn0",
    )(x, o, wo_bf, g, wg_bf, wu_bf, wd_bf)


def _glu_pointwise(x, g_ref, w1_ref, b1_ref, d):
    h = _rms(x, g_ref[...]).astype(BF16)
    u = jnp.dot(h, w1_ref[...], preferred_element_type=F32) + b1_ref[...]
    return u[:, :d] * jax.nn.sigmoid(u[:, d:])


def _ln_swish_pointwise(c, lng_ref, lnb_ref, w2_ref, b2_ref):
    mu = jnp.mean(c, axis=-1, keepdims=True)
    xc = c - mu
    yln = xc * lax.rsqrt(jnp.mean(xc * xc, axis=-1, keepdims=True) + EPS) * lng_ref[...] + lnb_ref[...]
    return jnp.dot(_silu(yln).astype(BF16), w2_ref[...], preferred_element_type=F32) + b2_ref[...]


def _conv_p_body(x_ref, g_ref, w1_ref, b1_ref, wdw_ref, bdw_ref, lng_ref, lnb_ref, w2_ref, b2_ref,
                 y_ref, st_ref, u_sc, sh_sc, c_sc, *, tt, width):
    t = pl.program_id(1)
    d = x_ref.shape[-1]
    keep = width - 1

    @pl.when(t == 0)
    def _():
        u_sc[0:CONV_HALO, :] = jnp.zeros((CONV_HALO, d), F32)

    x = x_ref[...]
    u_sc[CONV_HALO:CONV_HALO + tt, :] = _glu_pointwise(x, g_ref, w1_ref, b1_ref, d)

    base = CONV_HALO - keep
    span = sh_sc.shape[1]
    for b in range(1, SUBLANES):
        sh_sc[b - 1, :, :] = u_sc[b:b + span, :]

    def chunk(r, carry):
        r0 = pl.multiple_of(r * CONV_ROWS, CONV_ROWS)
        for c0 in range(0, d, LANES):
            lanes = slice(c0, c0 + LANES)
            acc = jnp.zeros((CONV_ROWS, LANES), F32)
            for j in range(width):
                a, b = divmod(base + j, SUBLANES)
                start = pl.multiple_of(r0 + SUBLANES * a, SUBLANES)
                if b == 0:
                    tap = u_sc[pl.ds(start, CONV_ROWS), lanes]
                else:
                    tap = sh_sc[b - 1, pl.ds(start, CONV_ROWS), lanes]
                acc = acc + wdw_ref[j:j + 1, lanes] * tap
            c_sc[pl.ds(r0, CONV_ROWS), lanes] = acc
        return carry

    lax.fori_loop(0, tt // CONV_ROWS, chunk, 0)

    c = c_sc[...] + bdw_ref[...]
    y_ref[...] = x + _ln_swish_pointwise(c, lng_ref, lnb_ref, w2_ref, b2_ref)
    st_ref[...] = u_sc[CONV_HALO + tt - keep:CONV_HALO + tt, :]
    u_sc[0:CONV_HALO, :] = u_sc[tt:tt + CONV_HALO, :]


def _conv_prompt(x, batch, seq, g, w1_bf, b1, wdw, bdw, lng, lnb, w2_bf, b2):
    n, d = x.shape
    width = wdw.shape[0]
    tt = _tile(CONV_TILE, seq)
    nt = seq // tt
    row = lambda b, t: (b * nt + t, 0)
    y, st = pl.pallas_call(
        functools.partial(_conv_p_body, tt=tt, width=width),
        grid=(batch, nt),
        in_specs=[
            pl.BlockSpec((tt, d), row),
            _const_spec((1, d)),
            _const_spec((d, 2 * d)),
            _const_spec((1, 2 * d)),
            _const_spec((width, d)),
            _const_spec((1, d)),
            _const_spec((1, d)),
            _const_spec((1, d)),
            _const_spec((d, d)),
            _const_spec((1, d)),
        ],
        out_specs=[
            pl.BlockSpec((tt, d), row),
            pl.BlockSpec((None, width - 1, d), lambda b, t: (b, 0, 0)),
        ],
        out_shape=[
            jax.ShapeDtypeStruct((n, d), F32),
            jax.ShapeDtypeStruct((batch, width - 1, d), F32),
        ],
        scratch_shapes=[
            pltpu.VMEM((CONV_HALO + tt, d), F32),
            pltpu.VMEM((SUBLANES - 1, CONV_HALO + tt - SUBLANES, d), F32),
            pltpu.VMEM((tt, d), F32),
        ],
        compiler_params=_cparams("parallel", "arbitrary"),
        name="conv_p",
    )(x, g, w1_bf, b1, wdw, bdw, lng, lnb, w2_bf, b2)
    return y, st


def _conv_s_body(x_ref, st_in_ref, g_ref, w1_ref, b1_ref, wdw_ref, bdw_ref, lng_ref, lnb_ref,
                 w2_ref, b2_ref, y_ref, st_out_ref, ext_sc, c_sc, *, s_new, width, n_seq):
    d = x_ref.shape[-1]
    keep = width - 1
    x = x_ref[...]
    ug = _glu_pointwise(x, g_ref, w1_ref, b1_ref, d)
    ext_sc[:, 0:keep, :] = st_in_ref[...]
    for s in range(n_seq):
        ext_sc[s, keep:keep + s_new, :] = ug[s * s_new:(s + 1) * s_new, :]
    acc = jnp.zeros((n_seq, s_new, d), F32)
    for j in range(width):
        acc = acc + wdw_ref[j:j + 1, :][None] * ext_sc[:, j:j + s_new, :]
    for s in range(n_seq):
        c_sc[s * s_new:(s + 1) * s_new, :] = acc[s]
    st_out_ref[...] = ext_sc[:, s_new:s_new + keep, :]
    c = c_sc[...] + bdw_ref[...]
    y_ref[...] = x + _ln_swish_pointwise(c, lng_ref, lnb_ref, w2_ref, b2_ref)


def _conv_sample(x, state, s_new, g, w1_bf, b1, wdw, bdw, lng, lnb, w2_bf, b2):
    n, d = x.shape
    bd = state.shape[0]
    width = wdw.shape[0]
    keep = width - 1
    ns = _tile(CONV_SEQ_BLOCK, bd)
    y, st = pl.pallas_call(
        functools.partial(_conv_s_body, s_new=s_new, width=width, n_seq=ns),
        grid=(bd // ns,),
        in_specs=[
            pl.BlockSpec((ns * s_new, d), lambda i: (i, 0)),
            pl.BlockSpec((ns, keep, d), lambda i: (i, 0, 0)),
            _const_spec((1, d)),
            _const_spec((d, 2 * d)),
            _const_spec((1, 2 * d)),
            _const_spec((width, d)),
            _const_spec((1, d)),
            _const_spec((1, d)),
            _const_spec((1, d)),
            _const_spec((d, d)),
            _const_spec((1, d)),
        ],
        out_specs=[
            pl.BlockSpec((ns * s_new, d), lambda i: (i, 0)),
            pl.BlockSpec((ns, keep, d), lambda i: (i, 0, 0)),
        ],
        out_shape=[
            jax.ShapeDtypeStruct((n, d), F32),
            jax.ShapeDtypeStruct((bd, keep, d), F32),
        ],
        scratch_shapes=[
            pltpu.VMEM((ns, keep + s_new, d), F32),
            pltpu.VMEM((ns * s_new, d), F32),
        ],
        compiler_params=_cparams("parallel"),
        name="conv_s",
    )(x, state, g, w1_bf, b1, wdw, bdw, lng, lnb, w2_bf, b2)
    return y, st


def _row_specs(tm, d, n_first_tiles):
    first = pl.BlockSpec((tm, d), lambda i, *_: (jnp.minimum(i, n_first_tiles - 1), 0))
    second = pl.BlockSpec((tm, d), lambda i, *_: (jnp.maximum(i - n_first_tiles, 0), 0))
    return first, second


def _rows(i, n_first_tiles, first_ref, second_ref):
    return jnp.where(i < n_first_tiles, first_ref[...], second_ref[...])


def _router_body(xa_ref, xb_ref, g_ref, wrt_ref, tri_ref, idx_ref, gate_ref, rank_ref, cnt_ref, cnt_sc,
                 *, n_first_tiles):
    i = pl.program_id(0)

    @pl.when(i == 0)
    def _():
        cnt_sc[...] = jnp.zeros(cnt_sc.shape, F32)

    h = _rms(_rows(i, n_first_tiles, xa_ref, xb_ref), g_ref[...])
    logits = lax.dot_general(wrt_ref[...], h, (((1,), (1,)), ((), ())),
                             precision=lax.Precision.HIGHEST,
                             preferred_element_type=F32)
    n_e = float(logits.shape[0])
    eidx = lax.broadcasted_iota(jnp.int32, logits.shape, 0).astype(F32)
    v1 = jnp.max(logits, axis=0, keepdims=True)
    i1 = jnp.min(jnp.where(logits == v1, eidx, n_e), axis=0, keepdims=True)
    rest = jnp.where(eidx == i1, -jnp.inf, logits)
    v2 = jnp.max(rest, axis=0, keepdims=True)
    i2 = jnp.min(jnp.where(rest == v2, eidx, n_e), axis=0, keepdims=True)
    e = jnp.exp(v2 - v1)
    g1 = 1.0 / (1.0 + e)
    idx_ref[0:1, :] = i1.astype(jnp.int32)
    idx_ref[1:2, :] = i2.astype(jnp.int32)
    gate_ref[0:1, :] = g1
    gate_ref[1:2, :] = e * g1

    hot1 = (eidx == i1).astype(F32)
    hot2 = (eidx == i2).astype(F32)
    tri = tri_ref[...]
    incl1 = jnp.dot(hot1.astype(BF16), tri, preferred_element_type=F32)
    incl2 = jnp.dot(hot2.astype(BF16), tri, preferred_element_type=F32)
    cnt = cnt_sc[...]
    c1 = jnp.sum(hot1, axis=1, keepdims=True)
    c2 = jnp.sum(hot2, axis=1, keepdims=True)
    r1 = jnp.sum(hot1 * (cnt + incl1), axis=0, keepdims=True) - 1.0
    r2 = jnp.sum(hot2 * (cnt + c1 + incl2), axis=0, keepdims=True) - 1.0
    rank_ref[0:1, :] = r1.astype(jnp.int32)
    rank_ref[1:2, :] = r2.astype(jnp.int32)
    cnt_new = cnt + c1 + c2
    cnt_sc[...] = cnt_new
    cnt_ref[...] = cnt_new.astype(jnp.int32)


def _router(xa, xb, g, w_router_t):
    d = xa.shape[1]
    n = xa.shape[0] + xb.shape[0]
    n_e = w_router_t.shape[0]
    tm = _tile(ROUTER_TILE, xb.shape[0])
    n_first = xa.shape[0] // _tile(tm, xa.shape[0])
    tri = (jnp.arange(tm)[:, None] <= jnp.arange(tm)[None, :]).astype(BF16)
    tok = pl.BlockSpec((TOP_K, tm), lambda i: (0, i))
    return pl.pallas_call(
        functools.partial(_router_body, n_first_tiles=n_first),
        grid=(n // tm,),
        in_specs=[
            *_row_specs(tm, d, n_first),
            _const_spec((1, d)),
            _const_spec((n_e, d)),
            _const_spec((tm, tm)),
        ],
        out_specs=[tok, tok, tok, pl.BlockSpec((n_e, 1), lambda i: (0, 0))],
        out_shape=[
            jax.ShapeDtypeStruct((TOP_K, n), jnp.int32),
            jax.ShapeDtypeStruct((TOP_K, n), F32),
            jax.ShapeDtypeStruct((TOP_K, n), jnp.int32),
            jax.ShapeDtypeStruct((n_e, 1), jnp.int32),
        ],
        scratch_shapes=[pltpu.VMEM((n_e, 1), F32)],
        compiler_params=_cparams("arbitrary"),
        name="router",
    )(xa, xb, g, w_router_t, tri)


def _dispatch_body(dest_ref, xa_ref, xb_ref, g_ref, gate_ref, xs_in, xs_out, buf_sc, sem,
                   *, tm, d, n_first_tiles):
    del xs_in
    i = pl.program_id(0)
    slot = lax.rem(i, 2)
    h = _rms(_rows(i, n_first_tiles, xa_ref, xb_ref), g_ref[...])
    gate = gate_ref[...]
    for k in range(TOP_K):
        buf_sc[slot, k, :, 0:d] = h
        buf_sc[slot, k, :, d:] = jnp.broadcast_to(gate[:, k:k + 1], (tm, GATE_LANES))

    def issue(t, carry):
        for k in range(TOP_K):
            dst = dest_ref[0, k * tm + t]
            pltpu.make_async_copy(buf_sc.at[slot, k, pl.ds(t, 1), :],
                                  xs_out.at[pl.ds(dst, 1), :], sem.at[slot]).start()
        return carry

    lax.fori_loop(0, tm, issue, 0, unroll=ROW_DMA_UNROLL)

    def wait_rows(which):
        for k in range(TOP_K):
            pltpu.make_async_copy(buf_sc.at[which, k], xs_out.at[pl.ds(0, tm), :],
                                  sem.at[which]).wait()

    @pl.when(i > 0)
    def _():
        wait_rows(1 - slot)

    @pl.when(i == pl.num_programs(0) - 1)
    def _():
        wait_rows(slot)


def _dispatch(xa, xb, g, gate_cols, dest_tiles, p_rows, tm):
    d = xa.shape[1]
    n = xa.shape[0] + xb.shape[0]
    n_first = xa.shape[0] // _tile(tm, xa.shape[0])
    dg = d + GATE_LANES
    xs0 = jnp.zeros((p_rows, dg), F32)
    return pl.pallas_call(
        functools.partial(_dispatch_body, tm=tm, d=d, n_first_tiles=n_first),
        grid=(n // tm,),
        in_specs=[
            pl.BlockSpec((None, 1, TOP_K * tm), lambda i: (i, 0, 0), memory_space=pltpu.SMEM),
            *_row_specs(tm, d, n_first),
            _const_spec((1, d)),
            pl.BlockSpec((tm, TOP_K), lambda i: (i, 0)),
            pl.BlockSpec(memory_space=pl.ANY),
        ],
        out_specs=pl.BlockSpec(memory_space=pl.ANY),
        out_shape=jax.ShapeDtypeStruct((p_rows, dg), F32),
        scratch_shapes=[
            pltpu.VMEM((2, TOP_K, tm, dg), F32),
            pltpu.SemaphoreType.DMA((2,)),
        ],
        input_output_aliases={5: 0},
        compiler_params=_cparams("arbitrary"),
        name="dispatch",
    )(dest_tiles, xa, xb, g, gate_cols, xs0)


def _moe_body(te_ref, tv_ref, ts_ref, x_ref, wg_ref, wu_ref, wd_ref, y_ref, xb_sc, acc_sc):
    t = pl.program_id(0)
    c = pl.program_id(1)
    d = acc_sc.shape[1]

    @pl.when(tv_ref[t] == 1)
    def _():
        @pl.when(c == 0)
        def _():
            acc_sc[...] = jnp.zeros(acc_sc.shape, F32)
            xb_sc[...] = x_ref[:, 0:d].astype(BF16)

        x = xb_sc[...]
        gate = jnp.dot(x, wg_ref[...].astype(BF16), preferred_element_type=F32)
        up = jnp.dot(x, wu_ref[...].astype(BF16), preferred_element_type=F32)
        a = (_silu(gate) * up).astype(BF16)
        acc_sc[...] += jnp.dot(a, wd_ref[...].astype(BF16), preferred_element_type=F32)

        @pl.when(c == pl.num_programs(1) - 1)
        def _():
            y_ref[...] = acc_sc[...] * x_ref[:, d:d + 1]

    @pl.when((tv_ref[t] == 0) & (c == pl.num_programs(1) - 1))
    def _():
        y_ref[...] = jnp.zeros(y_ref.shape, F32)


def _moe(xs, tile_expert, tile_valid, tile_src, w_gate_e, w_up_e, w_down_e, tm):
    p_rows, dg = xs.shape
    n_e, d, f = w_gate_e.shape
    fc = _tile(MOE_FF_CHUNK, f)
    nc = f // fc
    n_tiles = p_rows // tm

    def chunk_of(t, c, tv):
        return jnp.where(tv[t] == 1, c, nc - 1)

    grid_spec = pltpu.PrefetchScalarGridSpec(
        num_scalar_prefetch=3,
        grid=(n_tiles, nc),
        in_specs=[
            pl.BlockSpec((tm, dg), lambda t, c, te, tv, ts: (ts[t], 0)),
            pl.BlockSpec((None, d, fc), lambda t, c, te, tv, ts: (te[t], 0, chunk_of(t, c, tv))),
            pl.BlockSpec((None, d, fc), lambda t, c, te, tv, ts: (te[t], 0, chunk_of(t, c, tv))),
            pl.BlockSpec((None, fc, d), lambda t, c, te, tv, ts: (te[t], chunk_of(t, c, tv), 0)),
        ],
        out_specs=pl.BlockSpec((tm, d), lambda t, c, te, tv, ts: (t, 0)),
        scratch_shapes=[pltpu.VMEM((tm, d), BF16), pltpu.VMEM((tm, d), F32)],
    )
    return pl.pallas_call(
        _moe_body,
        grid_spec=grid_spec,
        out_shape=jax.ShapeDtypeStruct((p_rows, d), F32),
        compiler_params=_cparams("arbitrary", "arbitrary"),
        name="moe",
    )(tile_expert, tile_valid, tile_src, xs, w_gate_e, w_up_e, w_down_e)


def _combine_body(dest_ref, dest_next_ref, xa_ref, xb_ref, ys_hbm, oa_ref, ob_ref, buf_sc, sem,
                  *, tm, n_first_tiles):
    i = pl.program_id(0)
    n = pl.num_programs(0)
    slot = lax.rem(i, 2)

    def start_gather(dref, which):
        def issue(t, carry):
            for k in range(TOP_K):
                src = dref[0, k * tm + t]
                pltpu.make_async_copy(ys_hbm.at[pl.ds(src, 1), :],
                                      buf_sc.at[which, k, pl.ds(t, 1), :], sem.at[which]).start()
            return carry

        lax.fori_loop(0, tm, issue, 0, unroll=ROW_DMA_UNROLL)

    @pl.when(i == 0)
    def _():
        start_gather(dest_ref, slot)

    @pl.when(i + 1 < n)
    def _():
        start_gather(dest_next_ref, 1 - slot)

    for k in range(TOP_K):
        pltpu.make_async_copy(ys_hbm.at[pl.ds(0, tm), :], buf_sc.at[slot, k], sem.at[slot]).wait()
    y = _rows(i, n_first_tiles, xa_ref, xb_ref) + (buf_sc[slot, 0] + buf_sc[slot, 1])

    @pl.when(i < n_first_tiles)
    def _():
        oa_ref[...] = y

    @pl.when(i >= n_first_tiles)
    def _():
        ob_ref[...] = y


def _combine(xa, xb, ys, dest_tiles, tm):
    d = xa.shape[1]
    n_tiles = (xa.shape[0] + xb.shape[0]) // tm
    n_first = xa.shape[0] // _tile(tm, xa.shape[0])
    dest_spec = lambda f: pl.BlockSpec((None, 1, TOP_K * tm), lambda i: (f(i), 0, 0),
                                       memory_space=pltpu.SMEM)
    rows_a, rows_b = _row_specs(tm, d, n_first)
    return pl.pallas_call(
        functools.partial(_combine_body, tm=tm, n_first_tiles=n_first),
        grid=(n_tiles,),
        in_specs=[
            dest_spec(lambda i: i),
            dest_spec(lambda i: jnp.minimum(i + 1, n_tiles - 1)),
            rows_a, rows_b,
            pl.BlockSpec(memory_space=pl.ANY),
        ],
        out_specs=list(_row_specs(tm, d, n_first)),
        out_shape=[jax.ShapeDtypeStruct(xa.shape, F32), jax.ShapeDtypeStruct(xb.shape, F32)],
        scratch_shapes=[
            pltpu.VMEM((2, TOP_K, tm, d), F32),
            pltpu.SemaphoreType.DMA((2,)),
        ],
        compiler_params=_cparams("arbitrary"),
        name="combine",
    )(dest_tiles, dest_tiles, xa, xb, ys)


def _moe_layer(xa, xb, g, w_router, w_gate_e, w_up_e, w_down_e):
    d = xa.shape[1]
    n = xa.shape[0] + xb.shape[0]
    n_e = w_router.shape[1]
    tm = min(MOE_TILE, n)
    idx, gates, rank, counts = _router(xa, xb, g, w_router.T)

    counts = counts[:, 0]
    padded = ((counts + tm - 1) // tm) * tm
    ends = jnp.cumsum(padded)
    offs = ends - padded
    off_of = jnp.zeros_like(idx)
    for e in range(n_e):
        off_of = jnp.where(idx == e, offs[e], off_of)
    dest = off_of + rank
    n_tiles = (TOP_K * n) // tm + n_e
    p_rows = n_tiles * tm
    tile_start = jnp.arange(n_tiles, dtype=jnp.int32) * tm
    tile_valid = (tile_start < ends[-1]).astype(jnp.int32)
    last_valid = jnp.maximum(jnp.sum(tile_valid) - 1, 0)
    tile_src = jnp.minimum(jnp.arange(n_tiles, dtype=jnp.int32), last_valid)
    tile_expert = jnp.minimum(
        jnp.sum((tile_src * tm)[:, None] >= ends[None, :], axis=1), n_e - 1).astype(jnp.int32)

    td = _tile(DISPATCH_TILE, xb.shape[0])
    dest_tiles = dest.reshape(TOP_K, n // td, td).transpose(1, 0, 2).reshape(n // td, 1, TOP_K * td)
    xs = _dispatch(xa, xb, g, gates.T, dest_tiles, p_rows, td)
    ys = _moe(xs, tile_expert, tile_valid, tile_src, w_gate_e, w_up_e, w_down_e, tm)
    return _combine(xa, xb, ys, dest_tiles, td)


def kernel(x_prompt, x_sample, cache_k, cache_v, state_conv, page_table, norm_attn, w_qkv, q_norm, k_norm, lambda_q1, lambda_k1, lambda_q2, lambda_k2, subln_g, w_o, norm_ffn0, w_gate0, w_up0, w_down0, norm_conv, w_pw1, b_pw1, w_dw, b_dw, ln_g, ln_b, w_pw2, b_pw2, norm_ffn1, w_router, w_gate_e, w_up_e, w_down_e):
    batch, seq, d = x_prompt.shape
    bd, s_new, _ = x_sample.shape
    n_heads = cache_k.shape[2]
    head_dim = q_norm.shape[0]
    hw = 2 * head_dim
    n_p = batch * seq
    n_s = bd * s_new

    row = lambda v: v.reshape(1, -1)
    bf = lambda w: w.astype(BF16)
    xp = x_prompt.reshape(n_p, d)
    xs = x_sample.reshape(n_s, d)
    grp = jnp.kron(jnp.eye(d // head_dim, dtype=F32), jnp.ones((head_dim, head_dim), F32)).astype(BF16)
    qn_t = row(jnp.tile(q_norm, d // head_dim))
    kn_t = row(jnp.tile(k_norm, d // head_dim))
    lam_vecs = (row(lambda_q1), row(lambda_k1), row(lambda_q2), row(lambda_k2))
    sg = row(subln_g)
    w_qkv_bf = bf(w_qkv)

    qk_scale = head_dim ** -0.5
    qs, ks, vs, _, _ = _qkv(xs, row(norm_attn), w_qkv_bf, qn_t, kn_t, grp, head_dim, F32, qk_scale)
    qp, kp, vp, kpb, vpb, osm = _qkv_attn_sample(
        xp, row(norm_attn), w_qkv_bf, qn_t, kn_t, grp, qk_scale * math.log2(math.e),
        qs.reshape(bd, s_new, d), ks, vs, cache_k, cache_v, page_table, lam_vecs, sg, n_heads, head_dim)
    op = _attn_prompt(qp, kpb, vpb, lam_vecs, sg, batch, seq, n_heads, head_dim)

    ffn0_w = (bf(w_o), row(norm_ffn0), bf(w_gate0), bf(w_up0), bf(w_down0))
    xp = _ffn0(xp, op, *ffn0_w)
    xs = _ffn0(xs, osm.reshape(n_s, d), *ffn0_w)

    conv_w = (row(norm_conv), bf(w_pw1), row(b_pw1), w_dw, row(b_dw), row(ln_g), row(ln_b),
              bf(w_pw2), row(b_pw2))
    xp, conv_prompt = _conv_prompt(xp, batch, seq, *conv_w)
    xs, conv_sample = _conv_sample(xs, state_conv, s_new, *conv_w)

    xp, xs = _moe_layer(xp, xs, row(norm_ffn1), w_router, w_gate_e, w_up_e, w_down_e)

    return (xp.reshape(batch, seq, d), xs.reshape(bd, s_new, d),
            kp.reshape(batch, seq, n_heads, hw), vp.reshape(batch, seq, n_heads, hw),
            ks.reshape(bd, s_new, n_heads, hw), vs.reshape(bd, s_new, n_heads, hw),
            conv_prompt, conv_sample)
```
